```python
import math
import jax
import jax.numpy as jnp
from jax import lax
import numpy as np

D_MODEL = 4096
BATCH = 2
SEQ = 8192
DEPTH = 2

MLA_HEADS = 16
MLA_Q_RANK = 1024
MLA_KV_RANK = 512
MLA_NOPE_DIM = 128
MLA_ROPE_DIM = 64
MLA_V_DIM = 128
ROPE_THETA = 10000.0
Q_BLOCK = 128
DN_HEADS = 16
DN_K_DIM = 128
DN_V_DIM = 128
DN_CONV = 5
DN_CHUNK = 64
D_FF = 2 * D_MODEL
N_SUB = 3
ALPHA = (2.0 * DEPTH) ** 0.25
BETA_INIT = (8.0 * DEPTH) ** -0.25
EPS = 1e-6

MLA_QK_DIM = MLA_NOPE_DIM + MLA_ROPE_DIM
MLA_OUT_W = MLA_HEADS * MLA_V_DIM
DN_QK_W = DN_HEADS * DN_K_DIM
DN_V_W = DN_HEADS * DN_V_DIM
DN_QKV_W = 2 * DN_QK_W + DN_V_W
IN_SPLITS = (MLA_Q_RANK, MLA_KV_RANK, MLA_ROPE_DIM, DN_QKV_W, DN_V_W, 2 * DN_HEADS, 2 * DN_HEADS, D_MODEL, D_MODEL)
D_IN = sum(IN_SPLITS)

kernel_name = 'hybrid_mla_gdn_macaron_deepnorm_adaln'


def split_columns(t, sizes):
    out, start = [], 0
    for s in sizes:
        out.append(t[..., start:start + s])
        start += s
    return out


def layer_norm(x, g, b):
    xf = x.astype(jnp.float32)
    mu = jnp.mean(xf, axis=-1, keepdims=True)
    var = jnp.mean(jnp.square(xf - mu), axis=-1, keepdims=True)
    y = (xf - mu) * lax.rsqrt(var + EPS)
    return (y * g.astype(jnp.float32) + b.astype(jnp.float32)).astype(x.dtype)


def rms_norm(x, g):
    xf = x.astype(jnp.float32)
    y = xf * lax.rsqrt(jnp.mean(xf * xf, axis=-1, keepdims=True) + EPS)
    return (y * g.astype(jnp.float32)).astype(x.dtype)


def l2_normalize(x):
    return x * lax.rsqrt(jnp.sum(x * x, axis=-1, keepdims=True) + EPS)


def modulate(t, shift, scale):
    return t * (1.0 + scale[:, None, :]) + shift[:, None, :]


def rope(x, positions):
    half = MLA_ROPE_DIM // 2
    inv_freq = ROPE_THETA ** (-jnp.arange(half, dtype=jnp.float32) / half)
    ang = positions.astype(jnp.float32)[..., None] * inv_freq
    ang = ang.reshape(ang.shape[:2] + (1,) * (x.ndim - 3) + (half,))
    cos, sin = jnp.cos(ang), jnp.sin(ang)
    xf = x.astype(jnp.float32)
    x1, x2 = xf[..., :half], xf[..., half:]
    return jnp.concatenate([x1 * cos - x2 * sin, x2 * cos + x1 * sin], axis=-1).astype(x.dtype)


def swiglu(h, w_gate, w_up, w_down):
    return (jax.nn.silu(h @ w_gate) * (h @ w_up)) @ w_down


def mla_attention(q_lat, kv_lat, k_rope_in, positions, q_norm_g, kv_norm_g, w_uq, w_ukv):
    B, S, _ = q_lat.shape
    nb = S // Q_BLOCK
    q = (rms_norm(q_lat, q_norm_g) @ w_uq).reshape(B, S, MLA_HEADS, MLA_QK_DIM)
    q_nope = q[..., :MLA_NOPE_DIM]
    q_rope = rope(q[..., MLA_NOPE_DIM:], positions)
    kv = (rms_norm(kv_lat, kv_norm_g) @ w_ukv).reshape(B, S, MLA_HEADS, MLA_NOPE_DIM + MLA_V_DIM)
    k_nope, v = kv[..., :MLA_NOPE_DIM], kv[..., MLA_NOPE_DIM:]
    k_rope = rope(k_rope_in, positions)
    scale = MLA_QK_DIM ** -0.5

    def to_blocks(t):
        return jnp.moveaxis(t.reshape((B, nb, Q_BLOCK) + t.shape[2:]), 1, 0)

    def attend(blk):
        qn, qr = blk
        s = jnp.einsum('bqhd,bkhd->bhqk', qn, k_nope, preferred_element_type=jnp.float32)
        s = s + jnp.einsum('bqhr,bkr->bhqk', qr, k_rope, preferred_element_type=jnp.float32)
        p = jax.nn.softmax(s * scale, axis=-1).astype(v.dtype)
        return jnp.einsum('bhqk,bkhd->bqhd', p, v)

    o = lax.map(attend, (to_blocks(q_nope), to_blocks(q_rope)))
    return jnp.moveaxis(o, 0, 1).reshape(B, S, MLA_OUT_W)


def short_conv_silu(x, w):
    C = x.shape[-1]
    pad = DN_CONV // 2
    y = lax.conv_general_dilated(x, w[:, None, :], window_strides=(1,), padding=[(pad, pad)],
                                 dimension_numbers=('NWC', 'WIO', 'NWC'), feature_group_count=C)
    return jax.nn.silu(y)


def gated_delta_rule(q, k, v, g, beta):
    B, H, S, Dk = q.shape
    Dv = v.shape[-1]
    C = DN_CHUNK
    N = S // C
    q = q.reshape(B, H, N, C, Dk) * (Dk ** -0.5)
    k = k.reshape(B, H, N, C, Dk)
    v = v.reshape(B, H, N, C, Dv)
    beta = beta.reshape(B, H, N, C)
    G = jnp.cumsum(g.reshape(B, H, N, C), axis=-1)
    lower_incl = jnp.tril(jnp.ones((C, C), dtype=bool))
    lower_strict = jnp.tril(jnp.ones((C, C), dtype=bool), -1)
    decay = jnp.exp(jnp.where(lower_incl, G[..., :, None] - G[..., None, :], -jnp.inf))
    k_beta = k * beta[..., None]
    m = jnp.where(lower_strict, jnp.einsum('bhnid,bhnjd->bhnij', k_beta, k) * decay, 0.0)
    lhs = m + jnp.eye(C, dtype=jnp.float32)
    u = lax.linalg.triangular_solve(lhs, v * beta[..., None], left_side=True, lower=True)
    w = lax.linalg.triangular_solve(lhs, k_beta * jnp.exp(G)[..., None], left_side=True, lower=True)
    qk = jnp.einsum('bhnid,bhnjd->bhnij', q, k) * decay
    q_dec = q * jnp.exp(G)[..., None]
    k_dec = k * jnp.exp(G[..., -1:] - G)[..., None]
    g_last = jnp.exp(G[..., -1])

    def step(state, xs):
        u_c, w_c, qk_c, qd_c, kd_c, gl_c = xs
        v_new = u_c - jnp.einsum('bhcd,bhde->bhce', w_c, state)
        o_c = jnp.einsum('bhcd,bhde->bhce', qd_c, state) + jnp.einsum('bhij,bhje->bhie', qk_c, v_new)
        state = state * gl_c[..., None, None] + jnp.einsum('bhcd,bhce->bhde', kd_c, v_new)
        return state, o_c

    xs = tuple(jnp.moveaxis(t, 2, 0) for t in (u, w, qk, q_dec, k_dec, g_last))
    state0 = jnp.zeros((B, H, Dk, Dv), jnp.float32)
    _, o = lax.scan(step, state0, xs)
    return jnp.moveaxis(o, 0, 2).reshape(B, H, S, Dv)


def bidirectional_gated_deltanet(q, k, v, a_logits, b_logits, a_log, dt_bias):
    q = l2_normalize(q.astype(jnp.float32))
    k = l2_normalize(k.astype(jnp.float32))
    v = v.astype(jnp.float32)
    a_logits = a_logits.astype(jnp.float32)
    b_logits = b_logits.astype(jnp.float32)
    outs = []
    for d in range(2):
        g = -jnp.exp(a_log[d].astype(jnp.float32)) * jax.nn.softplus(a_logits[:, :, d] + dt_bias[d].astype(jnp.float32))
        beta = jax.nn.sigmoid(b_logits[:, :, d])
        args = (q, k, v, g, beta)
        if d == 1:
            args = tuple(jnp.flip(t, axis=1) for t in args)
        o = gated_delta_rule(*[jnp.moveaxis(t, 1, 2) for t in args])
        o = jnp.moveaxis(o, 2, 1)
        if d == 1:
            o = jnp.flip(o, axis=1)
        outs.append(o)
    return outs[0] + outs[1]


def hybrid_mixer(u, positions, w_in, q_norm_g, kv_norm_g, w_uq, w_ukv, w_branch_a,
                 conv_w, a_log, dt_bias, o_norm_g, w_branch_b, w_out):
    B, S, _ = u.shape
    q_lat, kv_lat, k_rope, dn_qkv, dn_z, dn_a, dn_b, gate_a, gate_b = split_columns(u @ w_in, IN_SPLITS)
    y_a = mla_attention(q_lat, kv_lat, k_rope, positions, q_norm_g, kv_norm_g, w_uq, w_ukv) @ w_branch_a
    qkv = short_conv_silu(dn_qkv, conv_w)
    dq = qkv[..., :DN_QK_W].reshape(B, S, DN_HEADS, DN_K_DIM)
    dk = qkv[..., DN_QK_W:2 * DN_QK_W].reshape(B, S, DN_HEADS, DN_K_DIM)
    dv = qkv[..., 2 * DN_QK_W:].reshape(B, S, DN_HEADS, DN_V_DIM)
    o_b = bidirectional_gated_deltanet(dq, dk, dv, dn_a.reshape(B, S, 2, DN_HEADS),
                                       dn_b.reshape(B, S, 2, DN_HEADS), a_log, dt_bias)
    o_b = rms_norm(o_b, o_norm_g) * jax.nn.silu(dn_z.reshape(B, S, DN_HEADS, DN_V_DIM).astype(jnp.float32))
    y_b = o_b.reshape(B, S, DN_V_W).astype(u.dtype) @ w_branch_b
    merged = jax.nn.sigmoid(gate_a) * y_a + jax.nn.sigmoid(gate_b) * y_b
    return merged @ w_out


def setup_inputs(seed: int = 0) -> dict:
    key = jax.random.key(seed)
    ks = iter(jax.random.split(key, 40))
    L = DEPTH

    def nrm(shape, std):
        return std * jax.random.normal(next(ks), shape, jnp.float32)

    def gain(shape):
        return 1.0 + nrm(shape, 0.02)

    x = nrm((BATCH, SEQ, D_MODEL), 1.0)
    c = nrm((BATCH, D_MODEL), 1.0)
    positions = jnp.arange(SEQ, dtype=jnp.int32)[None, :] + jax.random.randint(next(ks), (BATCH, 1), 0, 4096, dtype=jnp.int32)
    ln_in_g = gain((D_MODEL,))
    ln_in_b = nrm((D_MODEL,), 0.02)
    w_ada = nrm((D_MODEL, N_SUB * 3 * D_MODEL), 0.5 * D_MODEL ** -0.5)
    b_ada = nrm((N_SUB * 3 * D_MODEL,), 0.02)
    ada_table = nrm((L, N_SUB, 3, D_MODEL), 0.02) + jnp.array([0.0, 0.0, 1.0], jnp.float32)[None, None, :, None]
    ffn1_w_gate = nrm((L, D_MODEL, D_FF), D_MODEL ** -0.5)
    ffn1_w_up = nrm((L, D_MODEL, D_FF), D_MODEL ** -0.5)
    ffn1_w_down = nrm((L, D_FF, D_MODEL), BETA_INIT * D_FF ** -0.5)
    w_in = nrm((L, D_MODEL, D_IN), D_MODEL ** -0.5)
    mla_q_norm_g = gain((L, MLA_Q_RANK))
    mla_kv_norm_g = gain((L, MLA_KV_RANK))
    mla_w_uq = nrm((L, MLA_Q_RANK, MLA_HEADS * MLA_QK_DIM), MLA_Q_RANK ** -0.5)
    mla_w_ukv = nrm((L, MLA_KV_RANK, MLA_HEADS * (MLA_NOPE_DIM + MLA_V_DIM)), MLA_KV_RANK ** -0.5)
    w_branch_a = nrm((L, MLA_OUT_W, D_MODEL), MLA_OUT_W ** -0.5)
    dn_conv_w = nrm((L, DN_CONV, DN_QKV_W), DN_CONV ** -0.5)
    dn_a_log = jnp.log(jax.random.uniform(next(ks), (L, 2, DN_HEADS), jnp.float32, 1.0, 16.0))
    dt = jnp.exp(jax.random.uniform(next(ks), (L, 2, DN_HEADS), jnp.float32, math.log(1e-3), math.log(1e-1)))
    dn_dt_bias = dt + jnp.log(-jnp.expm1(-dt))
    dn_norm_g = gain((L, DN_V_DIM))
    w_branch_b = nrm((L, DN_V_W, D_MODEL), DN_V_W ** -0.5)
    w_out = nrm((L, D_MODEL, D_MODEL), BETA_INIT * D_MODEL ** -0.5)
    ffn2_w_gate = nrm((L, D_MODEL, D_FF), D_MODEL ** -0.5)
    ffn2_w_up = nrm((L, D_MODEL, D_FF), D_MODEL ** -0.5)
    ffn2_w_down = nrm((L, D_FF, D_MODEL), BETA_INIT * D_FF ** -0.5)
    post_ln_g = gain((L, N_SUB, D_MODEL))
    post_ln_b = nrm((L, N_SUB, D_MODEL), 0.02)
    return {'x': x, 'c': c, 'positions': positions, 'ln_in_g': ln_in_g, 'ln_in_b': ln_in_b,
            'w_ada': w_ada, 'b_ada': b_ada, 'ada_table': ada_table,
            'ffn1_w_gate': ffn1_w_gate, 'ffn1_w_up': ffn1_w_up, 'ffn1_w_down': ffn1_w_down,
            'w_in': w_in, 'mla_q_norm_g': mla_q_norm_g, 'mla_kv_norm_g': mla_kv_norm_g,
            'mla_w_uq': mla_w_uq, 'mla_w_ukv': mla_w_ukv, 'w_branch_a': w_branch_a,
            'dn_conv_w': dn_conv_w, 'dn_a_log': dn_a_log, 'dn_dt_bias': dn_dt_bias, 'dn_norm_g': dn_norm_g,
            'w_branch_b': w_branch_b, 'w_out': w_out,
            'ffn2_w_gate': ffn2_w_gate, 'ffn2_w_up': ffn2_w_up, 'ffn2_w_down': ffn2_w_down,
            'post_ln_g': post_ln_g, 'post_ln_b': post_ln_b}


def reference(x, c, positions, ln_in_g, ln_in_b, w_ada, b_ada, ada_table,
              ffn1_w_gate, ffn1_w_up, ffn1_w_down, w_in, mla_q_norm_g, mla_kv_norm_g,
              mla_w_uq, mla_w_ukv, w_branch_a, dn_conv_w, dn_a_log, dn_dt_bias, dn_norm_g,
              w_branch_b, w_out, ffn2_w_gate, ffn2_w_up, ffn2_w_down, post_ln_g, post_ln_b):
    B = x.shape[0]
    x = layer_norm(x, ln_in_g, ln_in_b)
    cond = (jax.nn.silu(c) @ w_ada + b_ada).reshape(B, N_SUB, 3, D_MODEL)
    for l in range(DEPTH):
        mod = cond + ada_table[l]
        y = swiglu(modulate(x, mod[:, 0, 0], mod[:, 0, 1]), ffn1_w_gate[l], ffn1_w_up[l], ffn1_w_down[l])
        x = layer_norm(ALPHA * x + 0.5 * mod[:, 0, 2, None, :] * y, post_ln_g[l, 0], post_ln_b[l, 0])
        y = hybrid_mixer(modulate(x, mod[:, 1, 0], mod[:, 1, 1]), positions, w_in[l],
                         mla_q_norm_g[l], mla_kv_norm_g[l], mla_w_uq[l], mla_w_ukv[l], w_branch_a[l],
                         dn_conv_w[l], dn_a_log[l], dn_dt_bias[l], dn_norm_g[l], w_branch_b[l], w_out[l])
        x = layer_norm(ALPHA * x + mod[:, 1, 2, None, :] * y, post_ln_g[l, 1], post_ln_b[l, 1])
        y = swiglu(modulate(x, mod[:, 2, 0], mod[:, 2, 1]), ffn2_w_gate[l], ffn2_w_up[l], ffn2_w_down[l])
        x = layer_norm(ALPHA * x + 0.5 * mod[:, 2, 2, None, :] * y, post_ln_g[l, 2], post_ln_b[l, 2])
    return x
```

```python
import functools
import math

import jax
import jax.numpy as jnp
from jax import lax
from jax.experimental import pallas as pl
from jax.experimental.pallas import tpu as pltpu

F32 = jnp.float32
BF16 = jnp.bfloat16

NOPE = 128
ROPE = 64
MLA_V = 128
MLA_QK = NOPE + ROPE
DN_K = 128
DN_V = 128
DN_CONV = 5
CHUNK = 64
N_SUB = 3
EPS = 1e-6
ROPE_THETA = 10000.0
LOG2E = math.log2(math.e)

LANES = 128
SUBLANES = 8
V7X_VMEM_LIMIT_BYTES = 56 * 1024 * 1024


def _tile(dim, pref):
    t = min(dim, pref)
    assert dim % t == 0, (dim, pref)
    return t


def _params(sem):
    return pltpu.CompilerParams(dimension_semantics=sem, vmem_limit_bytes=V7X_VMEM_LIMIT_BYTES)


def _dot(a, b):
    return jnp.dot(a, b, preferred_element_type=F32)


def _dot_nt(a, b):
    return lax.dot_general(a, b, (((1,), (1,)), ((), ())), preferred_element_type=F32)


def _dot_tn(a, b):
    return lax.dot_general(a, b, (((0,), (0,)), ((), ())), preferred_element_type=F32)


def _adaln_body(c_ref, w_ref, b_ref, o_ref):
    c = c_ref[...]
    h = (c * jax.nn.sigmoid(c)).astype(BF16)
    o_ref[...] = _dot(h, w_ref[...].astype(BF16)) + b_ref[...]


def _adaln(c, w_ada, b_ada):
    B, D = c.shape
    N = w_ada.shape[1]
    rows = max(SUBLANES, B)
    c_pad = jnp.zeros((rows, D), F32).at[:B].set(c)
    tn = _tile(N, 512)
    out = pl.pallas_call(
        _adaln_body,
        grid=(N // tn,),
        in_specs=[pl.BlockSpec((rows, D), lambda j: (0, 0)),
                  pl.BlockSpec((D, tn), lambda j: (0, j)),
                  pl.BlockSpec((1, tn), lambda j: (0, j))],
        out_specs=pl.BlockSpec((rows, tn), lambda j: (0, j)),
        out_shape=jax.ShapeDtypeStruct((rows, N), F32),
        compiler_params=_params(("parallel",)),
        name="adaln",
    )(c_pad, w_ada, b_ada.reshape(1, N))
    return out[:B]


def _post_body(*refs, has_y, has_mod, alpha, r):
    it = iter(refs)
    x_ref = next(it)
    if has_y:
        y_ref, gate_ref = next(it), next(it)
    g_ref, b_ref = next(it), next(it)
    if has_mod:
        sh_ref, sc_ref = next(it), next(it)
    xo_ref = next(it)
    if has_mod:
        h_ref = next(it)
    x = x_ref[...]
    if has_y:
        x = alpha * x + (r * gate_ref[0]) * y_ref[...]
    mu = jnp.mean(x, axis=-1, keepdims=True)
    xc = x - mu
    var = jnp.mean(xc * xc, axis=-1, keepdims=True)
    y = xc * lax.rsqrt(var + EPS) * g_ref[...] + b_ref[...]
    xo_ref[...] = y
    if has_mod:
        h_ref[...] = (y * (1.0 + sc_ref[0]) + sh_ref[0]).astype(BF16)


def _post(x, y, gate, g, b, shift, scale, *, seq, alpha, r):
    M, D = x.shape
    has_y = y is not None
    has_mod = shift is not None
    tm = _tile(seq, 256)
    per_b = seq // tm
    row = pl.BlockSpec((tm, D), lambda i: (i, 0))
    vec = pl.BlockSpec((1, D), lambda i: (0, 0))
    bvec = pl.BlockSpec((1, 1, D), lambda i: (i // per_b, 0, 0))
    args, specs = [x], [row]
    if has_y:
        args += [y, gate.reshape(-1, 1, D)]
        specs += [row, bvec]
    args += [g.reshape(1, D), b.reshape(1, D)]
    specs += [vec, vec]
    if has_mod:
        args += [shift.reshape(-1, 1, D), scale.reshape(-1, 1, D)]
        specs += [bvec, bvec]
    out_shape = [jax.ShapeDtypeStruct((M, D), F32)]
    out_specs = [row]
    if has_mod:
        out_shape.append(jax.ShapeDtypeStruct((M, D), BF16))
        out_specs.append(row)
    outs = pl.pallas_call(
        functools.partial(_post_body, has_y=has_y, has_mod=has_mod, alpha=alpha, r=r),
        grid=(M // tm,),
        in_specs=specs,
        out_specs=out_specs,
        out_shape=out_shape,
        compiler_params=_params(("parallel",)),
        name="post_ln",
    )(*args)
    return outs if has_mod else (outs[0], None)


def _mm_body(a_ref, w_ref, o_ref, *scratch, nk):
    if nk == 1:
        o_ref[...] = _dot(a_ref[...], w_ref[...]).astype(o_ref.dtype)
        return
    acc_ref, = scratch
    k = pl.program_id(2)

    @pl.when(k == 0)
    def _():
        acc_ref[...] = jnp.zeros_like(acc_ref)

    acc_ref[...] += _dot(a_ref[...], w_ref[...])

    @pl.when(k == nk - 1)
    def _():
        o_ref[...] = acc_ref[...].astype(o_ref.dtype)


def _matmul(a, w, out_dtype, *, tm=1024, tn=1024, tk=4096):
    M, K = a.shape
    N = w.shape[1]
    tm, tn, tk = _tile(M, tm), _tile(N, tn), _tile(K, tk)
    nk = K // tk
    return pl.pallas_call(
        functools.partial(_mm_body, nk=nk),
        grid=(M // tm, N // tn, nk),
        in_specs=[pl.BlockSpec((tm, tk), lambda i, j, k: (i, k)),
                  pl.BlockSpec((tk, tn), lambda i, j, k: (k, j))],
        out_specs=pl.BlockSpec((tm, tn), lambda i, j, k: (i, j)),
        out_shape=jax.ShapeDtypeStruct((M, N), out_dtype),
        scratch_shapes=[pltpu.VMEM((tm, tn), F32)] if nk > 1 else [],
        compiler_params=_params(("parallel", "parallel", "arbitrary")),
        name="matmul",
    )(a, w)


def _gateup_body(a_ref, wg_ref, wu_ref, o_ref):
    a = a_ref[...]
    g = _dot(a, wg_ref[...])
    u = _dot(a, wu_ref[...])
    o_ref[...] = (g * jax.nn.sigmoid(g) * u).astype(o_ref.dtype)


def _gateup(a, wg, wu):
    M, K = a.shape
    N = wg.shape[1]
    tm, tn = _tile(M, 1024), _tile(N, 512)
    return pl.pallas_call(
        _gateup_body,
        grid=(M // tm, N // tn),
        in_specs=[pl.BlockSpec((tm, K), lambda i, j: (i, 0)),
                  pl.BlockSpec((K, tn), lambda i, j: (0, j)),
                  pl.BlockSpec((K, tn), lambda i, j: (0, j))],
        out_specs=pl.BlockSpec((tm, tn), lambda i, j: (i, j)),
        out_shape=jax.ShapeDtypeStruct((M, N), BF16),
        compiler_params=_params(("parallel", "parallel")),
        name="ffn_gate_up",
    )(a, wg, wu)


def _merge_body(oa_ref, ob_ref, wa_ref, wb_ref, ga_ref, gb_ref, o_ref):
    ya = _dot(oa_ref[...], wa_ref[...])
    yb = _dot(ob_ref[...], wb_ref[...])
    o_ref[...] = (jax.nn.sigmoid(ga_ref[...]) * ya + jax.nn.sigmoid(gb_ref[...]) * yb).astype(o_ref.dtype)


def _merge(oa, ob, wa, wb, p2, d_model, gate_col0):
    M, Ka = oa.shape
    Kb = ob.shape[1]
    tm, tn = _tile(M, 1024), _tile(d_model, 512)
    ja = gate_col0 // tn
    jb = (gate_col0 + d_model) // tn
    return pl.pallas_call(
        _merge_body,
        grid=(M // tm, d_model // tn),
        in_specs=[pl.BlockSpec((tm, Ka), lambda i, j: (i, 0)),
                  pl.BlockSpec((tm, Kb), lambda i, j: (i, 0)),
                  pl.BlockSpec((Ka, tn), lambda i, j: (0, j)),
                  pl.BlockSpec((Kb, tn), lambda i, j: (0, j)),
                  pl.BlockSpec((tm, tn), lambda i, j: (i, ja + j)),
                  pl.BlockSpec((tm, tn), lambda i, j: (i, jb + j))],
        out_specs=pl.BlockSpec((tm, tn), lambda i, j: (i, j)),
        out_shape=jax.ShapeDtypeStruct((M, d_model), BF16),
        compiler_params=_params(("parallel", "parallel")),
        name="branch_merge",
    )(oa, ob, wa, wb, p2, p2)


def _rms(x, g):
    return x * lax.rsqrt(jnp.mean(x * x, axis=-1, keepdims=True) + EPS) * g


def _mla_prep_body(ql_ref, kvl_ref, kr_ref, tab_ref, gq_ref, gkv_ref, wq_ref, wkv_ref,
                   q_ref, k_ref, v_ref, *, heads, q_scale):
    qn = _rms(ql_ref[...], gq_ref[...]).astype(BF16)
    kvn = _rms(kvl_ref[...], gkv_ref[...]).astype(BF16)
    tab = tab_ref[...]
    tab_sin = pltpu.roll(tab, 64, 1)
    x = kr_ref[...]
    lane = lax.broadcasted_iota(jnp.int32, x.shape, 1)
    x_rot = jnp.where(lane < ROPE // 2, -pltpu.roll(x, LANES - ROPE // 2, 1), pltpu.roll(x, ROPE // 2, 1))
    k_rope = (x * tab + x_rot * tab_sin)[:, :ROPE].astype(BF16)
    for h in range(heads):
        r = _dot(qn, wq_ref[h])
        p = r[:, NOPE:] * tab
        roped = p + pltpu.roll(p, 64, 1)
        q_ref[0, h, :, :NOPE] = (r[:, :NOPE] * q_scale).astype(BF16)
        q_ref[0, h, :, NOPE:] = (roped[:, :ROPE] * q_scale).astype(BF16)
        kv = _dot(kvn, wkv_ref[h])
        k_ref[0, h, :, :NOPE] = kv[:, :NOPE].astype(BF16)
        k_ref[0, h, :, NOPE:] = k_rope
        v_ref[0, h] = kv[:, NOPE:].astype(BF16)


def _mla_prep(p1, tab, gq, gkv, wq, wkv, *, batch, seq):
    heads, q_rank, _ = wq.shape
    kv_rank = wkv.shape[1]
    ts = _tile(seq, 512)
    ns = seq // ts
    row = lambda b, i: b * ns + i
    return pl.pallas_call(
        functools.partial(_mla_prep_body, heads=heads, q_scale=MLA_QK ** -0.5 * LOG2E),
        grid=(batch, ns),
        in_specs=[pl.BlockSpec((ts, q_rank), lambda b, i: (row(b, i), 0)),
                  pl.BlockSpec((ts, kv_rank), lambda b, i: (row(b, i), q_rank // kv_rank)),
                  pl.BlockSpec((ts, LANES), lambda b, i: (row(b, i), (q_rank + kv_rank) // LANES)),
                  pl.BlockSpec((ts, LANES), lambda b, i: (row(b, i), 0)),
                  pl.BlockSpec((1, q_rank), lambda b, i: (0, 0)),
                  pl.BlockSpec((1, kv_rank), lambda b, i: (0, 0)),
                  pl.BlockSpec((heads, q_rank, 2 * LANES), lambda b, i: (0, 0, 0)),
                  pl.BlockSpec((heads, kv_rank, 2 * LANES), lambda b, i: (0, 0, 0))],
        out_specs=[pl.BlockSpec((1, heads, ts, MLA_QK), lambda b, i: (b, 0, i, 0)),
                   pl.BlockSpec((1, heads, ts, MLA_QK), lambda b, i: (b, 0, i, 0)),
                   pl.BlockSpec((1, heads, ts, MLA_V), lambda b, i: (b, 0, i, 0))],
        out_shape=[jax.ShapeDtypeStruct((batch, heads, seq, MLA_QK), BF16),
                   jax.ShapeDtypeStruct((batch, heads, seq, MLA_QK), BF16),
                   jax.ShapeDtypeStruct((batch, heads, seq, MLA_V), BF16)],
        compiler_params=_params(("parallel", "parallel")),
        name="mla_prep",
    )(p1, p1, p1, tab, gq.reshape(1, -1), gkv.reshape(1, -1), wq, wkv)


def _flash_body(q_ref, k_ref, v_ref, o_ref, m_ref, l_ref, acc_ref, *, nk):
    j = pl.program_id(3)

    @pl.when(j == 0)
    def _():
        m_ref[...] = jnp.full_like(m_ref, -jnp.inf)
        l_ref[...] = jnp.zeros_like(l_ref)
        acc_ref[...] = jnp.zeros_like(acc_ref)

    s = _dot_nt(q_ref[0, 0], k_ref[0, 0])
    m_prev = m_ref[...]
    m_new = jnp.maximum(m_prev, jnp.max(s, axis=1, keepdims=True))
    alpha = jnp.exp2(m_prev - m_new)
    p = jnp.exp2(s - m_new[:, :1])
    l_ref[...] = alpha * l_ref[...] + jnp.sum(p, axis=1, keepdims=True)
    acc_ref[...] = alpha * acc_ref[...] + _dot(p.astype(BF16), v_ref[0, 0])
    m_ref[...] = m_new

    @pl.when(j == nk - 1)
    def _():
        o_ref[0] = (acc_ref[...] / l_ref[...]).astype(o_ref.dtype)


def _flash(q, k, v):
    B, H, S, _ = q.shape
    tq, tk = _tile(S, 512), _tile(S, 512)
    nk = S // tk
    return pl.pallas_call(
        functools.partial(_flash_body, nk=nk),
        grid=(B, H, S // tq, nk),
        in_specs=[pl.BlockSpec((1, 1, tq, MLA_QK), lambda b, h, i, j: (b, h, i, 0)),
                  pl.BlockSpec((1, 1, tk, MLA_QK), lambda b, h, i, j: (b, h, j, 0)),
                  pl.BlockSpec((1, 1, tk, MLA_V), lambda b, h, i, j: (b, h, j, 0))],
        out_specs=pl.BlockSpec((1, tq, MLA_V), lambda b, h, i, j: (b, i, h)),
        out_shape=jax.ShapeDtypeStruct((B, S, H * MLA_V), BF16),
        scratch_shapes=[pltpu.VMEM((tq, LANES), F32), pltpu.VMEM((tq, LANES), F32),
                        pltpu.VMEM((tq, MLA_V), F32)],
        compiler_params=_params(("parallel", "parallel", "parallel", "arbitrary")),
        name="mla_flash",
    )(q, k, v)


def _dn_conv_body(cur_ref, prev_ref, next_ref, w_ref, o_ref, buf_ref, *, ts, ns, qk_blocks):
    i = pl.program_id(1)
    c = pl.program_id(2)
    pad = DN_CONV // 2
    buf_ref[0:SUBLANES] = jnp.where(i == 0, 0.0, prev_ref[...])
    buf_ref[SUBLANES:SUBLANES + ts] = cur_ref[...]
    buf_ref[SUBLANES + ts:] = jnp.where(i == ns - 1, 0.0, next_ref[...])
    w = w_ref[...]
    acc = w[0:1] * buf_ref[SUBLANES - pad:SUBLANES - pad + ts]
    for t in range(1, DN_CONV):
        acc = acc + w[t:t + 1] * buf_ref[SUBLANES - pad + t:SUBLANES - pad + t + ts]
    y = acc * jax.nn.sigmoid(acc)
    kind = c // qk_blocks
    for g in range(y.shape[1] // DN_K):
        yg = y[:, g * DN_K:(g + 1) * DN_K]
        inv = lax.rsqrt(jnp.sum(yg * yg, axis=-1, keepdims=True) + EPS)
        inv = jnp.where(kind < 2, inv, 1.0) * jnp.where(kind == 0, DN_K ** -0.5, 1.0)
        o_ref[:, g * DN_K:(g + 1) * DN_K] = (yg * inv).astype(o_ref.dtype)


def _dn_conv(p2, conv_w, *, batch, seq, heads):
    width = 3 * heads * DN_K
    ts = _tile(seq, 512)
    ns = seq // ts
    cb = 4 * DN_K
    qk_blocks = heads * DN_K // cb
    r8 = ts // SUBLANES
    last8 = batch * seq // SUBLANES - 1
    return pl.pallas_call(
        functools.partial(_dn_conv_body, ts=ts, ns=ns, qk_blocks=qk_blocks),
        grid=(batch, ns, width // cb),
        in_specs=[pl.BlockSpec((ts, cb), lambda b, i, c: (b * ns + i, c)),
                  pl.BlockSpec((SUBLANES, cb), lambda b, i, c: (jnp.maximum((b * ns + i) * r8 - 1, 0), c)),
                  pl.BlockSpec((SUBLANES, cb), lambda b, i, c: (jnp.minimum((b * ns + i + 1) * r8, last8), c)),
                  pl.BlockSpec((DN_CONV, cb), lambda b, i, c: (0, c))],
        out_specs=pl.BlockSpec((ts, cb), lambda b, i, c: (b * ns + i, c)),
        out_shape=jax.ShapeDtypeStruct((batch * seq, width), BF16),
        scratch_shapes=[pltpu.VMEM((ts + 2 * SUBLANES, cb), F32)],
        compiler_params=_params(("parallel", "parallel", "parallel")),
        name="dn_conv",
    )(p2, p2, p2, conv_w)


def _dn_gate_body(x_ref, alog_ref, dtb_ref, o_ref, *, heads):
    x = x_ref[...]
    z = x + dtb_ref[...]
    softplus = jnp.maximum(z, 0.0) + jnp.log1p(jnp.exp(-jnp.abs(z)))
    g = -jnp.exp(alog_ref[...]) * softplus
    lane = lax.broadcasted_iota(jnp.int32, x.shape, 1)
    zt = jnp.where(lane < LANES - 2 * heads, g, jax.nn.sigmoid(x)).T
    o_ref[0, 2 * heads:] = zt[LANES - 2 * heads:]
    pos = lax.broadcasted_iota(jnp.int32, (2 * heads, LANES), 1) % CHUNK
    fwd = lax.broadcasted_iota(jnp.int32, (2 * heads, LANES), 0) < heads
    for grp in range(zt.shape[1] // LANES):
        cols = slice(grp * LANES, (grp + 1) * LANES)
        gt = zt[LANES - 4 * heads:LANES - 2 * heads, cols]
        pre, suf = gt, gt
        s = 1
        while s < CHUNK:
            pre = pre + jnp.where(pos >= s, pltpu.roll(pre, s, 1), 0.0)
            suf = suf + jnp.where(pos < CHUNK - s, pltpu.roll(suf, LANES - s, 1), 0.0)
            s *= 2
        o_ref[0, :2 * heads, cols] = jnp.where(fwd, pre, suf)


def _dn_gates(p1, a_log, dt_bias, *, batch, seq, heads, col_block):
    ts = _tile(seq, 512)
    ns = seq // ts
    lo, hi = LANES - 4 * heads, LANES - 2 * heads
    alog = jnp.zeros((1, LANES), F32).at[0, lo:hi].set(a_log.reshape(-1))
    dtb = jnp.zeros((1, LANES), F32).at[0, lo:hi].set(dt_bias.reshape(-1))
    return pl.pallas_call(
        functools.partial(_dn_gate_body, heads=heads),
        grid=(batch, ns),
        in_specs=[pl.BlockSpec((ts, LANES), lambda b, i: (b * ns + i, col_block)),
                  pl.BlockSpec((1, LANES), lambda b, i: (0, 0)),
                  pl.BlockSpec((1, LANES), lambda b, i: (0, 0))],
        out_specs=pl.BlockSpec((1, 4 * heads, ts), lambda b, i: (b, 0, i)),
        out_shape=jax.ShapeDtypeStruct((batch, 4 * heads, seq), F32),
        compiler_params=_params(("parallel", "parallel")),
        name="dn_gates",
    )(p1, alog, dtb)


def _split_dot(a, b):
    ah, bh = a.astype(BF16), b.astype(BF16)
    al = (a - ah.astype(F32)).astype(BF16)
    bl = (b - bh.astype(F32)).astype(BF16)
    return _dot(ah, bh) + _dot(ah, bl) + _dot(al, bh)


def _unit_triangular_inverse(m, eye):
    y = -m
    p = eye + y
    for _ in range(int(math.log2(CHUNK)) - 1):
        y = _split_dot(y, y)
        p = p + _split_dot(p, y)
    return p


def _dn_local_body(q_ref, k_ref, v_ref, grow_ref, glast_ref,
                   u_ref, w_ref, qd_ref, kd_ref, qk_ref, *, chunks):
    ii = lax.broadcasted_iota(jnp.int32, (CHUNK, CHUNK), 0)
    jj = lax.broadcasted_iota(jnp.int32, (CHUNK, CHUNK), 1)
    eye = (ii == jj).astype(F32)

    def chunk(c, carry):
        rows = pl.ds(pl.multiple_of(c * CHUNK, CHUNK), CHUNK)
        q, k, v = q_ref[rows, :], k_ref[rows, :], v_ref[rows, :]
        qf, kf = q.astype(F32), k.astype(F32)
        qk_kk = _dot_nt(jnp.concatenate([q, k], axis=0), k)
        q_kt, k_kt = qk_kk[:CHUNK], qk_kk[CHUNK:]
        for d in range(2):
            incl = (jj <= ii) if d == 0 else (jj >= ii)
            strict = (jj < ii) if d == 0 else (jj > ii)
            r = jnp.broadcast_to(grow_ref[0, d, 0, pl.ds(c, 1), :], (LANES, LANES))
            rt = r.T
            g_col, b_col = rt[:CHUNK], rt[CHUNK:]
            g_row, b_row = r[:CHUNK, :CHUNK], r[:CHUNK, CHUNK:]
            g_last = jnp.broadcast_to(glast_ref[0, d, 0, pl.ds(c, 1), :], (CHUNK, LANES))
            decay = jnp.exp(jnp.where(incl, g_col[:, :CHUNK] - g_row, -jnp.inf))
            m = jnp.where(strict, k_kt * b_col[:, :CHUNK] * decay, 0.0)
            t_beta = _unit_triangular_inverse(m, eye) * b_row
            u_ref[0, d, 0, rows, :] = _dot(t_beta.astype(BF16), v)
            w_ref[0, d, 0, rows, :] = _dot((t_beta * jnp.exp(g_row)).astype(BF16), k).astype(BF16)
            qk_ref[0, d, 0, rows, :] = (q_kt * decay).astype(BF16)
            qd_ref[0, d, 0, rows, :] = (qf * jnp.exp(g_col)).astype(BF16)
            kd_ref[0, d, 0, rows, :] = (kf * jnp.exp(g_last - g_col)).astype(BF16)
        return carry

    lax.fori_loop(0, chunks, chunk, 0)


def _dn_local(qkv, grow, glast, *, batch, seq, heads):
    nc = seq // CHUNK
    cbk = _tile(nc, 8)
    L = cbk * CHUNK
    nb = nc // cbk
    tok = lambda off: pl.BlockSpec((L, DN_K), lambda b, h, t: (b * nb + t, off + h))
    gate = pl.BlockSpec((1, 2, 1, cbk, LANES), lambda b, h, t: (b, 0, h, t, 0))
    out = lambda width: pl.BlockSpec((1, 2, 1, L, width), lambda b, h, t: (b, 0, h, t, 0))
    shp = lambda width, dt: jax.ShapeDtypeStruct((batch, 2, heads, seq, width), dt)
    return pl.pallas_call(
        functools.partial(_dn_local_body, chunks=cbk),
        grid=(batch, heads, nb),
        in_specs=[tok(0), tok(heads), tok(2 * heads), gate, gate],
        out_specs=[out(DN_V), out(DN_K), out(DN_K), out(DN_K), out(CHUNK)],
        out_shape=[shp(DN_V, F32), shp(DN_K, BF16), shp(DN_K, BF16), shp(DN_K, BF16), shp(CHUNK, BF16)],
        compiler_params=_params(("parallel", "parallel", "parallel")),
        name="dn_local",
    )(qkv, qkv, qkv, grow, glast)


def _dn_scan_body(*refs, chunks, hb):
    ins, (of_ref, ob_ref, st_ref) = refs[:12], refs[12:]
    t = pl.program_id(2)

    @pl.when(t == 0)
    def _():
        st_ref[...] = jnp.zeros_like(st_ref)

    def step(ci, carry):
        for d in range(2):
            u_ref, w_ref, qd_ref, kd_ref, qk_ref, gl_ref = ins[6 * d:6 * d + 6]
            o_ref = of_ref if d == 0 else ob_ref
            c = ci if d == 0 else chunks - 1 - ci
            rows = pl.ds(pl.multiple_of(c * CHUNK, CHUNK), CHUNK)
            for j in range(hb):
                state = st_ref[d, j]
                sb = state.astype(BF16)
                v_new = u_ref[0, 0, j, rows, :] - _dot(w_ref[0, 0, j, rows, :], sb)
                vb = v_new.astype(BF16)
                o = _dot(qd_ref[0, 0, j, rows, :], sb) + _dot(qk_ref[0, 0, j, rows, :], vb)
                o_ref[rows, j * DN_V:(j + 1) * DN_V] = o
                decay = jnp.exp(gl_ref[0, 0, j, pl.ds(c, 1), :])
                st_ref[d, j] = state * decay + _dot_tn(kd_ref[0, 0, j, rows, :], vb)
        return carry

    lax.fori_loop(0, chunks, step, 0)


def _dn_scan(u, w, qd, kd, qk, glast, *, batch, seq, heads):
    nc = seq // CHUNK
    cbk = _tile(nc, 8)
    L = cbk * CHUNK
    nb = nc // cbk
    hb = 2 if heads % 2 == 0 else 1

    def spec(d, rows, width):
        if d == 0:
            return pl.BlockSpec((1, 1, hb, rows, width), lambda b, g, t: (b, 0, g, t, 0))
        return pl.BlockSpec((1, 1, hb, rows, width), lambda b, g, t: (b, 1, g, nb - 1 - t, 0))

    in_specs, args = [], []
    for d in range(2):
        in_specs += [spec(d, L, DN_V), spec(d, L, DN_K), spec(d, L, DN_K), spec(d, L, DN_K),
                     spec(d, L, CHUNK), spec(d, cbk, LANES)]
        args += [u, w, qd, kd, qk, glast]
    out_shape = jax.ShapeDtypeStruct((batch * seq, heads * DN_V), F32)
    return pl.pallas_call(
        functools.partial(_dn_scan_body, chunks=cbk, hb=hb),
        grid=(batch, heads // hb, nb),
        in_specs=in_specs,
        out_specs=[pl.BlockSpec((L, hb * DN_V), lambda b, g, t: (b * nb + t, g)),
                   pl.BlockSpec((L, hb * DN_V), lambda b, g, t: (b * nb + nb - 1 - t, g))],
        out_shape=[out_shape, out_shape],
        scratch_shapes=[pltpu.VMEM((2, hb, DN_K, DN_V), F32)],
        compiler_params=_params(("parallel", "parallel", "arbitrary")),
        name="dn_scan",
    )(*args)


def _dn_out_body(of_ref, ob_ref, z_ref, g_ref, o_ref):
    o = of_ref[...] + ob_ref[...]
    z = z_ref[...]
    g = g_ref[...]
    for h in range(o.shape[1] // DN_V):
        sl = slice(h * DN_V, (h + 1) * DN_V)
        oh, zh = o[:, sl], z[:, sl]
        y = oh * lax.rsqrt(jnp.mean(oh * oh, axis=-1, keepdims=True) + EPS) * g
        o_ref[:, sl] = (y * (zh * jax.nn.sigmoid(zh))).astype(o_ref.dtype)


def _dn_out(o_f, o_b, p2, norm_g, *, z_col0):
    M, W = o_f.shape
    tm, tw = _tile(M, 512), _tile(W, 512)
    zj = z_col0 // tw
    return pl.pallas_call(
        _dn_out_body,
        grid=(M // tm, W // tw),
        in_specs=[pl.BlockSpec((tm, tw), lambda i, j: (i, j)),
                  pl.BlockSpec((tm, tw), lambda i, j: (i, j)),
                  pl.BlockSpec((tm, tw), lambda i, j: (i, zj + j)),
                  pl.BlockSpec((1, DN_V), lambda i, j: (0, 0))],
        out_specs=pl.BlockSpec((tm, tw), lambda i, j: (i, j)),
        out_shape=jax.ShapeDtypeStruct((M, W), BF16),
        compiler_params=_params(("parallel", "parallel")),
        name="dn_out",
    )(o_f, o_b, p2, norm_g.reshape(1, DN_V))


def _rope_table(positions):
    half = ROPE // 2
    inv_freq = ROPE_THETA ** (-jnp.arange(half, dtype=F32) / half)
    ang = positions.astype(F32)[..., None] * inv_freq
    cos, sin = jnp.cos(ang), jnp.sin(ang)
    return jnp.concatenate([cos, cos, sin, sin], axis=-1).reshape(-1, 4 * half)


def _mixer(u, tab, w_in, q_norm_g, kv_norm_g, w_uq, w_ukv, w_branch_a, conv_w, a_log, dt_bias,
           o_norm_g, w_branch_b, w_out, *, batch, seq):
    d_model = u.shape[1]
    q_rank, kv_rank = q_norm_g.shape[0], kv_norm_g.shape[0]
    mla_heads = w_uq.shape[1] // MLA_QK
    dn_heads = a_log.shape[1]
    dn_qk_w, dn_v_w = dn_heads * DN_K, dn_heads * DN_V
    c_qkv = q_rank + kv_rank + ROPE
    c_z = c_qkv + 2 * dn_qk_w + dn_v_w
    c_a = c_z + dn_v_w
    c_gate = c_a + 4 * dn_heads
    assert ROPE + 4 * dn_heads == LANES and c_qkv - ROPE == (q_rank + kv_rank)
    w1 = jnp.concatenate([w_in[:, :c_qkv], w_in[:, c_a:c_gate]], axis=1).astype(BF16)
    w2 = jnp.concatenate([w_in[:, c_qkv:c_a], w_in[:, c_gate:]], axis=1).astype(BF16)
    p1 = _matmul(u, w1, F32, tm=512, tn=w1.shape[1])
    p2 = _matmul(u, w2, F32)

    wq = w_uq.reshape(q_rank, mla_heads, MLA_QK)
    r1, r2 = wq[..., NOPE:NOPE + ROPE // 2], wq[..., NOPE + ROPE // 2:]
    wq = jnp.concatenate([wq, -r2, r1], axis=-1).transpose(1, 0, 2).astype(BF16)
    wkv = w_ukv.reshape(kv_rank, mla_heads, NOPE + MLA_V).transpose(1, 0, 2).astype(BF16)
    q, k, v = _mla_prep(p1, tab, q_norm_g, kv_norm_g, wq, wkv, batch=batch, seq=seq)
    o_a = _flash(q, k, v).reshape(batch * seq, mla_heads * MLA_V)

    qkv = _dn_conv(p2, conv_w, batch=batch, seq=seq, heads=dn_heads)
    gates = _dn_gates(p1, a_log, dt_bias, batch=batch, seq=seq, heads=dn_heads,
                      col_block=(q_rank + kv_rank) // LANES)
    nc = seq // CHUNK
    gates = gates.reshape(batch, 2, 2, dn_heads, nc, CHUNK)
    grow = gates.transpose(0, 2, 3, 4, 1, 5).reshape(batch, 2, dn_heads, nc, 2 * CHUNK)
    g_cum = gates[:, 0]
    glast = jnp.stack([g_cum[:, 0, :, :, CHUNK - 1], g_cum[:, 1, :, :, 0]], axis=1)
    glast = jnp.broadcast_to(glast[..., None], glast.shape + (LANES,))
    u_, w_, qd, kd, qk = _dn_local(qkv, grow, glast, batch=batch, seq=seq, heads=dn_heads)
    o_f, o_b = _dn_scan(u_, w_, qd, kd, qk, glast, batch=batch, seq=seq, heads=dn_heads)
    o_bn = _dn_out(o_f, o_b, p2, o_norm_g, z_col0=2 * dn_qk_w + dn_v_w)

    merged = _merge(o_a, o_bn, w_branch_a.astype(BF16), w_branch_b.astype(BF16), p2, d_model,
                    gate_col0=2 * dn_qk_w + 2 * dn_v_w)
    return _matmul(merged, w_out.astype(BF16), F32)


def _ffn(h, w_gate, w_up, w_down):
    a = _gateup(h, w_gate.astype(BF16), w_up.astype(BF16))
    return _matmul(a, w_down.astype(BF16), F32)


def kernel(x, c, positions, ln_in_g, ln_in_b, w_ada, b_ada, ada_table, ffn1_w_gate, ffn1_w_up, ffn1_w_down, w_in, mla_q_norm_g, mla_kv_norm_g, mla_w_uq, mla_w_ukv, w_branch_a, dn_conv_w, dn_a_log, dn_dt_bias, dn_norm_g, w_branch_b, w_out, ffn2_w_gate, ffn2_w_up, ffn2_w_down, post_ln_g, post_ln_b):
    B, S, D = x.shape
    depth = ada_table.shape[0]
    alpha = (2.0 * depth) ** 0.25
    cond = _adaln(c, w_ada, b_ada).reshape(B, N_SUB, 3, D)
    mod = cond[None] + ada_table[:, None]
    tab = _rope_table(positions)
    post = functools.partial(_post, seq=S, alpha=alpha)

    xf, h = post(x.reshape(B * S, D), None, None, ln_in_g, ln_in_b, mod[0, :, 0, 0], mod[0, :, 0, 1], r=1.0)
    for l in range(depth):
        y = _ffn(h, ffn1_w_gate[l], ffn1_w_up[l], ffn1_w_down[l])
        xf, h = post(xf, y, mod[l, :, 0, 2], post_ln_g[l, 0], post_ln_b[l, 0],
                     mod[l, :, 1, 0], mod[l, :, 1, 1], r=0.5)
        y = _mixer(h, tab, w_in[l], mla_q_norm_g[l], mla_kv_norm_g[l], mla_w_uq[l], mla_w_ukv[l],
                   w_branch_a[l], dn_conv_w[l], dn_a_log[l], dn_dt_bias[l], dn_norm_g[l],
                   w_branch_b[l], w_out[l], batch=B, seq=S)
        xf, h = post(xf, y, mod[l, :, 1, 2], post_ln_g[l, 1], post_ln_b[l, 1],
                     mod[l, :, 2, 0], mod[l, :, 2, 1], r=1.0)
        y = _ffn(h, ffn2_w_gate[l], ffn2_w_up[l], ffn2_w_down[l])
        last = l == depth - 1
        xf, h = post(xf, y, mod[l, :, 2, 2], post_ln_g[l, 2], post_ln_b[l, 2],
                     None if last else mod[l + 1, :, 0, 0], None if last else mod[l + 1, :, 0, 1], r=0.5)
    return xf.reshape(B, S, D)
```

```python
import functools
import math

import jax
import jax.numpy as jnp
from jax import lax
from jax.experimental import pallas as pl
from jax.experimental.pallas import tpu as pltpu

F32 = jnp.float32
BF16 = jnp.bfloat16

NOPE = 128
ROPE = 64
MLA_V = 128
MLA_QK = NOPE + ROPE
DN_K = 128
DN_V = 128
DN_CONV = 5
CHUNK = 64
N_SUB = 3
EPS = 1e-6
ROPE_THETA = 10000.0
LOG2E = math.log2(math.e)

LANES = 128
SUBLANES = 8
V7X_VMEM_LIMIT_BYTES = 56 * 1024 * 1024


def _tile(dim, pref):
    t = min(dim, pref)
    assert dim % t == 0, (dim, pref)
    return t


def _params(sem):
    return pltpu.CompilerParams(dimension_semantics=sem, vmem_limit_bytes=V7X_VMEM_LIMIT_BYTES)


def _dot(a, b):
    return jnp.dot(a, b, preferred_element_type=F32)


def _dot_nt(a, b):
    return lax.dot_general(a, b, (((1,), (1,)), ((), ())), preferred_element_type=F32)


def _dot_tn(a, b):
    return lax.dot_general(a, b, (((0,), (0,)), ((), ())), preferred_element_type=F32)


def _adaln_body(c_ref, w_ref, b_ref, o_ref):
    c = c_ref[...]
    h = (c * jax.nn.sigmoid(c)).astype(BF16)
    o_ref[...] = _dot(h, w_ref[...].astype(BF16)) + b_ref[...]


def _adaln(c, w_ada, b_ada):
    B, D = c.shape
    N = w_ada.shape[1]
    rows = max(SUBLANES, B)
    c_pad = jnp.zeros((rows, D), F32).at[:B].set(c)
    tn = _tile(N, 512)
    out = pl.pallas_call(
        _adaln_body,
        grid=(N // tn,),
        in_specs=[pl.BlockSpec((rows, D), lambda j: (0, 0)),
                  pl.BlockSpec((D, tn), lambda j: (0, j)),
                  pl.BlockSpec((1, tn), lambda j: (0, j))],
        out_specs=pl.BlockSpec((rows, tn), lambda j: (0, j)),
        out_shape=jax.ShapeDtypeStruct((rows, N), F32),
        compiler_params=_params(("parallel",)),
        name="adaln",
    )(c_pad, w_ada, b_ada.reshape(1, N))
    return out[:B]


def _post_body(*refs, has_y, has_mod, alpha, r):
    it = iter(refs)
    x_ref = next(it)
    if has_y:
        y_ref, gate_ref = next(it), next(it)
    g_ref, b_ref = next(it), next(it)
    if has_mod:
        sh_ref, sc_ref = next(it), next(it)
    xo_ref = next(it)
    if has_mod:
        h_ref = next(it)
    x = x_ref[...]
    if has_y:
        x = alpha * x + (r * gate_ref[0]) * y_ref[...]
    mu = jnp.mean(x, axis=-1, keepdims=True)
    xc = x - mu
    var = jnp.mean(xc * xc, axis=-1, keepdims=True)
    y = xc * lax.rsqrt(var + EPS) * g_ref[...] + b_ref[...]
    xo_ref[...] = y
    if has_mod:
        h_ref[...] = (y * (1.0 + sc_ref[0]) + sh_ref[0]).astype(BF16)


def _post(x, y, gate, g, b, shift, scale, *, seq, alpha, r):
    M, D = x.shape
    has_y = y is not None
    has_mod = shift is not None
    tm = _tile(seq, 256)
    per_b = seq // tm
    row = pl.BlockSpec((tm, D), lambda i: (i, 0))
    vec = pl.BlockSpec((1, D), lambda i: (0, 0))
    bvec = pl.BlockSpec((1, 1, D), lambda i: (i // per_b, 0, 0))
    args, specs = [x], [row]
    if has_y:
        args += [y, gate.reshape(-1, 1, D)]
        specs += [row, bvec]
    args += [g.reshape(1, D), b.reshape(1, D)]
    specs += [vec, vec]
    if has_mod:
        args += [shift.reshape(-1, 1, D), scale.reshape(-1, 1, D)]
        specs += [bvec, bvec]
    out_shape = [jax.ShapeDtypeStruct((M, D), F32)]
    out_specs = [row]
    if has_mod:
        out_shape.append(jax.ShapeDtypeStruct((M, D), BF16))
        out_specs.append(row)
    outs = pl.pallas_call(
        functools.partial(_post_body, has_y=has_y, has_mod=has_mod, alpha=alpha, r=r),
        grid=(M // tm,),
        in_specs=specs,
        out_specs=out_specs,
        out_shape=out_shape,
        compiler_params=_params(("parallel",)),
        name="post_ln",
    )(*args)
    return outs if has_mod else (outs[0], None)


def _mm_body(a_ref, w_ref, o_ref, *scratch, nk):
    if nk == 1:
        o_ref[...] = _dot(a_ref[...], w_ref[...]).astype(o_ref.dtype)
        return
    acc_ref, = scratch
    k = pl.program_id(2)

    @pl.when(k == 0)
    def _():
        acc_ref[...] = jnp.zeros_like(acc_ref)

    acc_ref[...] += _dot(a_ref[...], w_ref[...])

    @pl.when(k == nk - 1)
    def _():
        o_ref[...] = acc_ref[...].astype(o_ref.dtype)


def _matmul(a, w, out_dtype, *, tm=1024, tn=1024, tk=4096):
    M, K = a.shape
    N = w.shape[1]
    tm, tn, tk = _tile(M, tm), _tile(N, tn), _tile(K, tk)
    nk = K // tk
    return pl.pallas_call(
        functools.partial(_mm_body, nk=nk),
        grid=(M // tm, N // tn, nk),
        in_specs=[pl.BlockSpec((tm, tk), lambda i, j, k: (i, k)),
                  pl.BlockSpec((tk, tn), lambda i, j, k: (k, j))],
        out_specs=pl.BlockSpec((tm, tn), lambda i, j, k: (i, j)),
        out_shape=jax.ShapeDtypeStruct((M, N), out_dtype),
        scratch_shapes=[pltpu.VMEM((tm, tn), F32)] if nk > 1 else [],
        compiler_params=_params(("parallel", "parallel", "arbitrary")),
        name="matmul",
    )(a, w)


def _gateup_body(a_ref, wg_ref, wu_ref, o_ref):
    a = a_ref[...]
    g = _dot(a, wg_ref[...])
    u = _dot(a, wu_ref[...])
    o_ref[...] = (g * jax.nn.sigmoid(g) * u).astype(o_ref.dtype)


def _gateup(a, wg, wu):
    M, K = a.shape
    N = wg.shape[1]
    tm, tn = _tile(M, 1024), _tile(N, 512)
    return pl.pallas_call(
        _gateup_body,
        grid=(M // tm, N // tn),
        in_specs=[pl.BlockSpec((tm, K), lambda i, j: (i, 0)),
                  pl.BlockSpec((K, tn), lambda i, j: (0, j)),
                  pl.BlockSpec((K, tn), lambda i, j: (0, j))],
        out_specs=pl.BlockSpec((tm, tn), lambda i, j: (i, j)),
        out_shape=jax.ShapeDtypeStruct((M, N), BF16),
        compiler_params=_params(("parallel", "parallel")),
        name="ffn_gate_up",
    )(a, wg, wu)


def _merge_body(oa_ref, ob_ref, wa_ref, wb_ref, ga_ref, gb_ref, o_ref):
    ya = _dot(oa_ref[...], wa_ref[...])
    yb = _dot(ob_ref[...], wb_ref[...])
    o_ref[...] = (jax.nn.sigmoid(ga_ref[...]) * ya + jax.nn.sigmoid(gb_ref[...]) * yb).astype(o_ref.dtype)


def _merge(oa, ob, wa, wb, p2, d_model, gate_col0):
    M, Ka = oa.shape
    Kb = ob.shape[1]
    tm, tn = _tile(M, 1024), _tile(d_model, 512)
    ja = gate_col0 // tn
    jb = (gate_col0 + d_model) // tn
    return pl.pallas_call(
        _merge_body,
        grid=(M // tm, d_model // tn),
        in_specs=[pl.BlockSpec((tm, Ka), lambda i, j: (i, 0)),
                  pl.BlockSpec((tm, Kb), lambda i, j: (i, 0)),
                  pl.BlockSpec((Ka, tn), lambda i, j: (0, j)),
                  pl.BlockSpec((Kb, tn), lambda i, j: (0, j)),
                  pl.BlockSpec((tm, tn), lambda i, j: (i, ja + j)),
                  pl.BlockSpec((tm, tn), lambda i, j: (i, jb + j))],
        out_specs=pl.BlockSpec((tm, tn), lambda i, j: (i, j)),
        out_shape=jax.ShapeDtypeStruct((M, d_model), BF16),
        compiler_params=_params(("parallel", "parallel")),
        name="branch_merge",
    )(oa, ob, wa, wb, p2, p2)


def _rms(x, g):
    return x * lax.rsqrt(jnp.mean(x * x, axis=-1, keepdims=True) + EPS) * g


def _mla_prep_body(ql_ref, kvl_ref, kr_ref, tab_ref, gq_ref, gkv_ref, wq_ref, wkv_ref,
                   q_ref, k_ref, v_ref, *, heads, q_scale):
    qn = _rms(ql_ref[...], gq_ref[...]).astype(BF16)
    kvn = _rms(kvl_ref[...], gkv_ref[...]).astype(BF16)
    tab = tab_ref[...]
    tab_sin = pltpu.roll(tab, 64, 1)
    x = kr_ref[...]
    lane = lax.broadcasted_iota(jnp.int32, x.shape, 1)
    x_rot = jnp.where(lane < ROPE // 2, -pltpu.roll(x, LANES - ROPE // 2, 1), pltpu.roll(x, ROPE // 2, 1))
    k_rope = (x * tab + x_rot * tab_sin)[:, :ROPE].astype(BF16)
    for h in range(heads):
        r = _dot(qn, wq_ref[h])
        p = r[:, NOPE:] * tab
        roped = p + pltpu.roll(p, 64, 1)
        q_ref[0, h, :, :NOPE] = (r[:, :NOPE] * q_scale).astype(BF16)
        q_ref[0, h, :, NOPE:] = (roped[:, :ROPE] * q_scale).astype(BF16)
        kv = _dot(kvn, wkv_ref[h])
        k_ref[0, h, :, :NOPE] = kv[:, :NOPE].astype(BF16)
        k_ref[0, h, :, NOPE:] = k_rope
        v_ref[0, h] = kv[:, NOPE:].astype(BF16)


def _mla_prep(p1, tab, gq, gkv, wq, wkv, *, batch, seq):
    heads, q_rank, _ = wq.shape
    kv_rank = wkv.shape[1]
    ts = _tile(seq, 512)
    ns = seq // ts
    row = lambda b, i: b * ns + i
    return pl.pallas_call(
        functools.partial(_mla_prep_body, heads=heads, q_scale=MLA_QK ** -0.5 * LOG2E),
        grid=(batch, ns),
        in_specs=[pl.BlockSpec((ts, q_rank), lambda b, i: (row(b, i), 0)),
                  pl.BlockSpec((ts, kv_rank), lambda b, i: (row(b, i), q_rank // kv_rank)),
                  pl.BlockSpec((ts, LANES), lambda b, i: (row(b, i), (q_rank + kv_rank) // LANES)),
                  pl.BlockSpec((ts, LANES), lambda b, i: (row(b, i), 0)),
                  pl.BlockSpec((1, q_rank), lambda b, i: (0, 0)),
                  pl.BlockSpec((1, kv_rank), lambda b, i: (0, 0)),
                  pl.BlockSpec((heads, q_rank, 2 * LANES), lambda b, i: (0, 0, 0)),
                  pl.BlockSpec((heads, kv_rank, 2 * LANES), lambda b, i: (0, 0, 0))],
        out_specs=[pl.BlockSpec((1, heads, ts, MLA_QK), lambda b, i: (b, 0, i, 0)),
                   pl.BlockSpec((1, heads, ts, MLA_QK), lambda b, i: (b, 0, i, 0)),
                   pl.BlockSpec((1, heads, ts, MLA_V), lambda b, i: (b, 0, i, 0))],
        out_shape=[jax.ShapeDtypeStruct((batch, heads, seq, MLA_QK), BF16),
                   jax.ShapeDtypeStruct((batch, heads, seq, MLA_QK), BF16),
                   jax.ShapeDtypeStruct((batch, heads, seq, MLA_V), BF16)],
        compiler_params=_params(("parallel", "parallel")),
        name="mla_prep",
    )(p1, p1, p1, tab, gq.reshape(1, -1), gkv.reshape(1, -1), wq, wkv)


def _flash_body(q_ref, k_ref, v_ref, o_ref, s_ref, m_ref, l_ref, acc_ref, *, tk, nk):
    m_ref[...] = jnp.full_like(m_ref, -jnp.inf)
    l_ref[...] = jnp.zeros_like(l_ref)
    acc_ref[...] = jnp.zeros_like(acc_ref)
    q = q_ref[0, 0]

    def key_rows(j):
        return pl.ds(pl.multiple_of(j * tk, tk), tk)

    def scores(j):
        return _dot_nt(q, k_ref[0, 0, key_rows(j), :])

    s_ref[0] = scores(0)

    def block_pair(jj, carry):
        for u in range(2):
            j = 2 * jj + u
            s = s_ref[u]
            s_ref[1 - u] = scores(j + 1 if u == 0 else jnp.minimum(j + 1, nk - 1))
            m_prev = m_ref[...]
            m_new = jnp.maximum(m_prev, jnp.max(s, axis=1, keepdims=True))
            alpha = jnp.exp2(m_prev - m_new)
            p = jnp.exp2(s - m_new[:, :1])
            l_ref[...] = alpha * l_ref[...] + jnp.sum(p, axis=1, keepdims=True)
            acc_ref[...] = alpha * acc_ref[...] + _dot(p.astype(BF16), v_ref[0, 0, key_rows(j), :])
            m_ref[...] = m_new
        return carry

    lax.fori_loop(0, nk // 2, block_pair, 0)
    o_ref[0] = (acc_ref[...] / l_ref[...]).astype(o_ref.dtype)


def _flash(q, k, v):
    B, H, S, _ = q.shape
    tq, tk = _tile(S, 512), _tile(S, 512)
    if (S // tk) % 2:
        tk //= 2
    return pl.pallas_call(
        functools.partial(_flash_body, tk=tk, nk=S // tk),
        grid=(B, H, S // tq),
        in_specs=[pl.BlockSpec((1, 1, tq, MLA_QK), lambda b, h, i: (b, h, i, 0)),
                  pl.BlockSpec((1, 1, S, MLA_QK), lambda b, h, i: (b, h, 0, 0)),
                  pl.BlockSpec((1, 1, S, MLA_V), lambda b, h, i: (b, h, 0, 0))],
        out_specs=pl.BlockSpec((1, tq, MLA_V), lambda b, h, i: (b, i, h)),
        out_shape=jax.ShapeDtypeStruct((B, S, H * MLA_V), BF16),
        scratch_shapes=[pltpu.VMEM((2, tq, tk), F32), pltpu.VMEM((tq, LANES), F32),
                        pltpu.VMEM((tq, LANES), F32), pltpu.VMEM((tq, MLA_V), F32)],
        compiler_params=_params(("parallel", "parallel", "arbitrary")),
        name="mla_flash",
    )(q, k, v)


def _dn_conv_body(cur_ref, prev_ref, next_ref, w_ref, o_ref, buf_ref, *, ts, ns, qk_blocks):
    i = pl.program_id(1)
    c = pl.program_id(2)
    pad = DN_CONV // 2
    buf_ref[0:SUBLANES] = jnp.where(i == 0, 0.0, prev_ref[...])
    buf_ref[SUBLANES:SUBLANES + ts] = cur_ref[...]
    buf_ref[SUBLANES + ts:] = jnp.where(i == ns - 1, 0.0, next_ref[...])
    w = w_ref[...]
    acc = w[0:1] * buf_ref[SUBLANES - pad:SUBLANES - pad + ts]
    for t in range(1, DN_CONV):
        acc = acc + w[t:t + 1] * buf_ref[SUBLANES - pad + t:SUBLANES - pad + t + ts]
    y = acc * jax.nn.sigmoid(acc)
    kind = c // qk_blocks
    for g in range(y.shape[1] // DN_K):
        yg = y[:, g * DN_K:(g + 1) * DN_K]
        inv = lax.rsqrt(jnp.sum(yg * yg, axis=-1, keepdims=True) + EPS)
        inv = jnp.where(kind < 2, inv, 1.0) * jnp.where(kind == 0, DN_K ** -0.5, 1.0)
        o_ref[:, g * DN_K:(g + 1) * DN_K] = (yg * inv).astype(o_ref.dtype)


def _dn_conv(p2, conv_w, *, batch, seq, heads):
    width = 3 * heads * DN_K
    ts = _tile(seq, 512)
    ns = seq // ts
    cb = 4 * DN_K
    qk_blocks = heads * DN_K // cb
    r8 = ts // SUBLANES
    last8 = batch * seq // SUBLANES - 1
    return pl.pallas_call(
        functools.partial(_dn_conv_body, ts=ts, ns=ns, qk_blocks=qk_blocks),
        grid=(batch, ns, width // cb),
        in_specs=[pl.BlockSpec((ts, cb), lambda b, i, c: (b * ns + i, c)),
                  pl.BlockSpec((SUBLANES, cb), lambda b, i, c: (jnp.maximum((b * ns + i) * r8 - 1, 0), c)),
                  pl.BlockSpec((SUBLANES, cb), lambda b, i, c: (jnp.minimum((b * ns + i + 1) * r8, last8), c)),
                  pl.BlockSpec((DN_CONV, cb), lambda b, i, c: (0, c))],
        out_specs=pl.BlockSpec((ts, cb), lambda b, i, c: (b * ns + i, c)),
        out_shape=jax.ShapeDtypeStruct((batch * seq, width), BF16),
        scratch_shapes=[pltpu.VMEM((ts + 2 * SUBLANES, cb), F32)],
        compiler_params=_params(("parallel", "parallel", "parallel")),
        name="dn_conv",
    )(p2, p2, p2, conv_w)


def _dn_gate_body(x_ref, alog_ref, dtb_ref, o_ref, *, heads):
    x = x_ref[...]
    z = x + dtb_ref[...]
    softplus = jnp.maximum(z, 0.0) + jnp.log1p(jnp.exp(-jnp.abs(z)))
    g = -jnp.exp(alog_ref[...]) * softplus
    lane = lax.broadcasted_iota(jnp.int32, x.shape, 1)
    zt = jnp.where(lane < LANES - 2 * heads, g, jax.nn.sigmoid(x)).T
    o_ref[0, 2 * heads:] = zt[LANES - 2 * heads:]
    pos = lax.broadcasted_iota(jnp.int32, (2 * heads, LANES), 1) % CHUNK
    fwd = lax.broadcasted_iota(jnp.int32, (2 * heads, LANES), 0) < heads
    for grp in range(zt.shape[1] // LANES):
        cols = slice(grp * LANES, (grp + 1) * LANES)
        gt = zt[LANES - 4 * heads:LANES - 2 * heads, cols]
        pre, suf = gt, gt
        s = 1
        while s < CHUNK:
            pre = pre + jnp.where(pos >= s, pltpu.roll(pre, s, 1), 0.0)
            suf = suf + jnp.where(pos < CHUNK - s, pltpu.roll(suf, LANES - s, 1), 0.0)
            s *= 2
        o_ref[0, :2 * heads, cols] = jnp.where(fwd, pre, suf)


def _dn_gates(p1, a_log, dt_bias, *, batch, seq, heads, col_block):
    ts = _tile(seq, 512)
    ns = seq // ts
    lo, hi = LANES - 4 * heads, LANES - 2 * heads
    alog = jnp.zeros((1, LANES), F32).at[0, lo:hi].set(a_log.reshape(-1))
    dtb = jnp.zeros((1, LANES), F32).at[0, lo:hi].set(dt_bias.reshape(-1))
    return pl.pallas_call(
        functools.partial(_dn_gate_body, heads=heads),
        grid=(batch, ns),
        in_specs=[pl.BlockSpec((ts, LANES), lambda b, i: (b * ns + i, col_block)),
                  pl.BlockSpec((1, LANES), lambda b, i: (0, 0)),
                  pl.BlockSpec((1, LANES), lambda b, i: (0, 0))],
        out_specs=pl.BlockSpec((1, 4 * heads, ts), lambda b, i: (b, 0, i)),
        out_shape=jax.ShapeDtypeStruct((batch, 4 * heads, seq), F32),
        compiler_params=_params(("parallel", "parallel")),
        name="dn_gates",
    )(p1, alog, dtb)


def _split_dot(a, b):
    ah, bh = a.astype(BF16), b.astype(BF16)
    al = (a - ah.astype(F32)).astype(BF16)
    bl = (b - bh.astype(F32)).astype(BF16)
    return _dot(ah, bh) + _dot(ah, bl) + _dot(al, bh)


def _unit_triangular_inverses(ms, eye):
    a = [eye + m for m in ms]
    ab = [x.astype(BF16) for x in a]
    t = [eye - m for m in ms]
    for _ in range(int(math.log2(CHUNK)) - 2):
        r = [2.0 * eye - _dot(x, y.astype(BF16)) for x, y in zip(ab, t)]
        t = [_dot(x.astype(BF16), y.astype(BF16)) for x, y in zip(t, r)]
    r = [eye - _split_dot(x, y) for x, y in zip(a, t)]
    return [x + _dot(x.astype(BF16), y.astype(BF16)) for x, y in zip(t, r)]


def _dn_local_body(q_ref, k_ref, v_ref, grow_ref, glast_ref,
                   u_ref, w_ref, qd_ref, kd_ref, qk_ref, *, chunks):
    ii = lax.broadcasted_iota(jnp.int32, (CHUNK, CHUNK), 0)
    jj = lax.broadcasted_iota(jnp.int32, (CHUNK, CHUNK), 1)
    eye = (ii == jj).astype(F32)
    rows = [slice(c * CHUNK, (c + 1) * CHUNK) for c in range(chunks)]
    q = [q_ref[r, :] for r in rows]
    k = [k_ref[r, :] for r in rows]
    v = [v_ref[r, :] for r in rows]
    qk_kk = [_dot_nt(jnp.concatenate([x, y], axis=0), y) for x, y in zip(q, k)]

    chains = [(c, d) for c in range(chunks) for d in range(2)]
    g_row, b_row, g_col, g_last, decay, ms = [], [], [], [], [], []
    for c, d in chains:
        incl = (jj <= ii) if d == 0 else (jj >= ii)
        strict = (jj < ii) if d == 0 else (jj > ii)
        r = jnp.broadcast_to(grow_ref[0, d, 0, c:c + 1, :], (LANES, LANES))
        rt = r.T
        gc, bc = rt[:CHUNK], rt[CHUNK:]
        gr = r[:CHUNK, :CHUNK]
        dec = jnp.exp(jnp.where(incl, gc[:, :CHUNK] - gr, -jnp.inf))
        ms.append(jnp.where(strict, qk_kk[c][CHUNK:] * bc[:, :CHUNK] * dec, 0.0))
        g_row.append(gr)
        b_row.append(r[:CHUNK, CHUNK:])
        g_col.append(gc)
        g_last.append(jnp.broadcast_to(glast_ref[0, d, 0, c:c + 1, :], (CHUNK, LANES)))
        decay.append(dec)

    t_beta = [t * b for t, b in zip(_unit_triangular_inverses(ms, eye), b_row)]
    us = [_dot(t.astype(BF16), v[c]) for t, (c, d) in zip(t_beta, chains)]
    ws = [_dot((t * jnp.exp(g)).astype(BF16), k[c]) for t, g, (c, d) in zip(t_beta, g_row, chains)]
    for i, (c, d) in enumerate(chains):
        u_ref[0, d, 0, rows[c], :] = us[i]
        w_ref[0, d, 0, rows[c], :] = ws[i].astype(BF16)
        qk_ref[0, d, 0, rows[c], :] = (qk_kk[c][:CHUNK] * decay[i]).astype(BF16)
        qd_ref[0, d, 0, rows[c], :] = (q[c].astype(F32) * jnp.exp(g_col[i])).astype(BF16)
        kd_ref[0, d, 0, rows[c], :] = (k[c].astype(F32) * jnp.exp(g_last[i] - g_col[i])).astype(BF16)


def _dn_local(qkv, grow, glast, *, batch, seq, heads):
    nc = seq // CHUNK
    cbk = _tile(nc, 8)
    L = cbk * CHUNK
    nb = nc // cbk
    tok = lambda off: pl.BlockSpec((L, DN_K), lambda b, h, t: (b * nb + t, off + h))
    gate = pl.BlockSpec((1, 2, 1, cbk, LANES), lambda b, h, t: (b, 0, h, t, 0))
    out = lambda width: pl.BlockSpec((1, 2, 1, L, width), lambda b, h, t: (b, 0, h, t, 0))
    shp = lambda width, dt: jax.ShapeDtypeStruct((batch, 2, heads, seq, width), dt)
    return pl.pallas_call(
        functools.partial(_dn_local_body, chunks=cbk),
        grid=(batch, heads, nb),
        in_specs=[tok(0), tok(heads), tok(2 * heads), gate, gate],
        out_specs=[out(DN_V), out(DN_K), out(DN_K), out(DN_K), out(CHUNK)],
        out_shape=[shp(DN_V, F32), shp(DN_K, BF16), shp(DN_K, BF16), shp(DN_K, BF16), shp(CHUNK, BF16)],
        compiler_params=_params(("parallel", "parallel", "parallel")),
        name="dn_local",
    )(qkv, qkv, qkv, grow, glast)


def _dn_scan_body(*refs, chunks, hb):
    ins, (of_ref, ob_ref, st_ref) = refs[:12], refs[12:]
    t = pl.program_id(2)

    @pl.when(t == 0)
    def _():
        st_ref[...] = jnp.zeros_like(st_ref)

    chains = [(d, j) for d in range(2) for j in range(hb)]
    states = [st_ref[d, j] for d, j in chains]
    for ci in range(chunks):
        blk = []
        for d, j in chains:
            u_ref, w_ref, qd_ref, kd_ref, qk_ref, gl_ref = ins[6 * d:6 * d + 6]
            c = ci if d == 0 else chunks - 1 - ci
            rows = slice(c * CHUNK, (c + 1) * CHUNK)
            blk.append((u_ref[0, 0, j, rows, :], w_ref[0, 0, j, rows, :], qd_ref[0, 0, j, rows, :],
                        kd_ref[0, 0, j, rows, :], qk_ref[0, 0, j, rows, :],
                        jnp.exp(gl_ref[0, 0, j, c:c + 1, :]), rows))
        sb = [s.astype(BF16) for s in states]
        w_s = [_dot(b[1], s) for b, s in zip(blk, sb)]
        q_s = [_dot(b[2], s) for b, s in zip(blk, sb)]
        vb = [(b[0] - x).astype(BF16) for b, x in zip(blk, w_s)]
        outs = [x + _dot(b[4], y) for b, x, y in zip(blk, q_s, vb)]
        states = [s * b[5] + _dot_tn(b[3], y) for b, s, y in zip(blk, states, vb)]
        for (d, j), b, o in zip(chains, blk, outs):
            o_ref = of_ref if d == 0 else ob_ref
            o_ref[b[6], j * DN_V:(j + 1) * DN_V] = o
    for (d, j), s in zip(chains, states):
        st_ref[d, j] = s


def _dn_scan(u, w, qd, kd, qk, glast, *, batch, seq, heads):
    nc = seq // CHUNK
    cbk = _tile(nc, 8)
    L = cbk * CHUNK
    nb = nc // cbk
    hb = _tile(heads, 4)

    def spec(d, rows, width):
        if d == 0:
            return pl.BlockSpec((1, 1, hb, rows, width), lambda b, g, t: (b, 0, g, t, 0))
        return pl.BlockSpec((1, 1, hb, rows, width), lambda b, g, t: (b, 1, g, nb - 1 - t, 0))

    in_specs, args = [], []
    for d in range(2):
        in_specs += [spec(d, L, DN_V), spec(d, L, DN_K), spec(d, L, DN_K), spec(d, L, DN_K),
                     spec(d, L, CHUNK), spec(d, cbk, LANES)]
        args += [u, w, qd, kd, qk, glast]
    out_shape = jax.ShapeDtypeStruct((batch * seq, heads * DN_V), F32)
    return pl.pallas_call(
        functools.partial(_dn_scan_body, chunks=cbk, hb=hb),
        grid=(batch, heads // hb, nb),
        in_specs=in_specs,
        out_specs=[pl.BlockSpec((L, hb * DN_V), lambda b, g, t: (b * nb + t, g)),
                   pl.BlockSpec((L, hb * DN_V), lambda b, g, t: (b * nb + nb - 1 - t, g))],
        out_shape=[out_shape, out_shape],
        scratch_shapes=[pltpu.VMEM((2, hb, DN_K, DN_V), F32)],
        compiler_params=_params(("parallel", "parallel", "arbitrary")),
        name="dn_scan",
    )(*args)


def _dn_out_body(of_ref, ob_ref, z_ref, g_ref, o_ref):
    o = of_ref[...] + ob_ref[...]
    z = z_ref[...]
    g = g_ref[...]
    for h in range(o.shape[1] // DN_V):
        sl = slice(h * DN_V, (h + 1) * DN_V)
        oh, zh = o[:, sl], z[:, sl]
        y = oh * lax.rsqrt(jnp.mean(oh * oh, axis=-1, keepdims=True) + EPS) * g
        o_ref[:, sl] = (y * (zh * jax.nn.sigmoid(zh))).astype(o_ref.dtype)


def _dn_out(o_f, o_b, p2, norm_g, *, z_col0):
    M, W = o_f.shape
    tm, tw = _tile(M, 512), _tile(W, 512)
    zj = z_col0 // tw
    return pl.pallas_call(
        _dn_out_body,
        grid=(M // tm, W // tw),
        in_specs=[pl.BlockSpec((tm, tw), lambda i, j: (i, j)),
                  pl.BlockSpec((tm, tw), lambda i, j: (i, j)),
                  pl.BlockSpec((tm, tw), lambda i, j: (i, zj + j)),
                  pl.BlockSpec((1, DN_V), lambda i, j: (0, 0))],
        out_specs=pl.BlockSpec((tm, tw), lambda i, j: (i, j)),
        out_shape=jax.ShapeDtypeStruct((M, W), BF16),
        compiler_params=_params(("parallel", "parallel")),
        name="dn_out",
    )(o_f, o_b, p2, norm_g.reshape(1, DN_V))


def _rope_table(positions):
    half = ROPE // 2
    inv_freq = ROPE_THETA ** (-jnp.arange(half, dtype=F32) / half)
    ang = positions.astype(F32)[..., None] * inv_freq
    cos, sin = jnp.cos(ang), jnp.sin(ang)
    return jnp.concatenate([cos, cos, sin, sin], axis=-1).reshape(-1, 4 * half)


def _mixer(u, tab, w_in, q_norm_g, kv_norm_g, w_uq, w_ukv, w_branch_a, conv_w, a_log, dt_bias,
           o_norm_g, w_branch_b, w_out, *, batch, seq):
    d_model = u.shape[1]
    q_rank, kv_rank = q_norm_g.shape[0], kv_norm_g.shape[0]
    mla_heads = w_uq.shape[1] // MLA_QK
    dn_heads = a_log.shape[1]
    dn_qk_w, dn_v_w = dn_heads * DN_K, dn_heads * DN_V
    c_qkv = q_rank + kv_rank + ROPE
    c_z = c_qkv + 2 * dn_qk_w + dn_v_w
    c_a = c_z + dn_v_w
    c_gate = c_a + 4 * dn_heads
    assert ROPE + 4 * dn_heads == LANES and c_qkv - ROPE == (q_rank + kv_rank)
    w1 = jnp.concatenate([w_in[:, :c_qkv], w_in[:, c_a:c_gate]], axis=1).astype(BF16)
    w2 = jnp.concatenate([w_in[:, c_qkv:c_a], w_in[:, c_gate:]], axis=1).astype(BF16)
    p1 = _matmul(u, w1, F32, tm=512, tn=w1.shape[1])
    p2 = _matmul(u, w2, F32)

    wq = w_uq.reshape(q_rank, mla_heads, MLA_QK)
    r1, r2 = wq[..., NOPE:NOPE + ROPE // 2], wq[..., NOPE + ROPE // 2:]
    wq = jnp.concatenate([wq, -r2, r1], axis=-1).transpose(1, 0, 2).astype(BF16)
    wkv = w_ukv.reshape(kv_rank, mla_heads, NOPE + MLA_V).transpose(1, 0, 2).astype(BF16)
    q, k, v = _mla_prep(p1, tab, q_norm_g, kv_norm_g, wq, wkv, batch=batch, seq=seq)
    o_a = _flash(q, k, v).reshape(batch * seq, mla_heads * MLA_V)

    qkv = _dn_conv(p2, conv_w, batch=batch, seq=seq, heads=dn_heads)
    gates = _dn_gates(p1, a_log, dt_bias, batch=batch, seq=seq, heads=dn_heads,
                      col_block=(q_rank + kv_rank) // LANES)
    nc = seq // CHUNK
    gates = gates.reshape(batch, 2, 2, dn_heads, nc, CHUNK)
    grow = gates.transpose(0, 2, 3, 4, 1, 5).reshape(batch, 2, dn_heads, nc, 2 * CHUNK)
    g_cum = gates[:, 0]
    glast = jnp.stack([g_cum[:, 0, :, :, CHUNK - 1], g_cum[:, 1, :, :, 0]], axis=1)
    glast = jnp.broadcast_to(glast[..., None], glast.shape + (LANES,))
    u_, w_, qd, kd, qk = _dn_local(qkv, grow, glast, batch=batch, seq=seq, heads=dn_heads)
    o_f, o_b = _dn_scan(u_, w_, qd, kd, qk, glast, batch=batch, seq=seq, heads=dn_heads)
    o_bn = _dn_out(o_f, o_b, p2, o_norm_g, z_col0=2 * dn_qk_w + dn_v_w)

    merged = _merge(o_a, o_bn, w_branch_a.astype(BF16), w_branch_b.astype(BF16), p2, d_model,
                    gate_col0=2 * dn_qk_w + 2 * dn_v_w)
    return _matmul(merged, w_out.astype(BF16), F32)


def _ffn(h, w_gate, w_up, w_down):
    a = _gateup(h, w_gate.astype(BF16), w_up.astype(BF16))
    return _matmul(a, w_down.astype(BF16), F32)


def kernel(x, c, positions, ln_in_g, ln_in_b, w_ada, b_ada, ada_table, ffn1_w_gate, ffn1_w_up, ffn1_w_down, w_in, mla_q_norm_g, mla_kv_norm_g, mla_w_uq, mla_w_ukv, w_branch_a, dn_conv_w, dn_a_log, dn_dt_bias, dn_norm_g, w_branch_b, w_out, ffn2_w_gate, ffn2_w_up, ffn2_w_down, post_ln_g, post_ln_b):
    B, S, D = x.shape
    depth = ada_table.shape[0]
    alpha = (2.0 * depth) ** 0.25
    cond = _adaln(c, w_ada, b_ada).reshape(B, N_SUB, 3, D)
    mod = cond[None] + ada_table[:, None]
    tab = _rope_table(positions)
    post = functools.partial(_post, seq=S, alpha=alpha)

    xf, h = post(x.reshape(B * S, D), None, None, ln_in_g, ln_in_b, mod[0, :, 0, 0], mod[0, :, 0, 1], r=1.0)
    for l in range(depth):
        y = _ffn(h, ffn1_w_gate[l], ffn1_w_up[l], ffn1_w_down[l])
        xf, h = post(xf, y, mod[l, :, 0, 2], post_ln_g[l, 0], post_ln_b[l, 0],
                     mod[l, :, 1, 0], mod[l, :, 1, 1], r=0.5)
        y = _mixer(h, tab, w_in[l], mla_q_norm_g[l], mla_kv_norm_g[l], mla_w_uq[l], mla_w_ukv[l],
                   w_branch_a[l], dn_conv_w[l], dn_a_log[l], dn_dt_bias[l], dn_norm_g[l],
                   w_branch_b[l], w_out[l], batch=B, seq=S)
        xf, h = post(xf, y, mod[l, :, 1, 2], post_ln_g[l, 1], post_ln_b[l, 1],
                     mod[l, :, 2, 0], mod[l, :, 2, 1], r=1.0)
        y = _ffn(h, ffn2_w_gate[l], ffn2_w_up[l], ffn2_w_down[l])
        last = l == depth - 1
        xf, h = post(xf, y, mod[l, :, 2, 2], post_ln_g[l, 2], post_ln_b[l, 2],
                     None if last else mod[l + 1, :, 0, 0], None if last else mod[l + 1, :, 0, 1], r=0.5)
    return xf.reshape(B, S, D)
```

```python
import functools
import math

import jax
import jax.numpy as jnp
from jax import lax
from jax.experimental import pallas as pl
from jax.experimental.pallas import tpu as pltpu

F32 = jnp.float32
BF16 = jnp.bfloat16

NOPE = 128
ROPE = 64
MLA_V = 128
MLA_QK = NOPE + ROPE
DN_K = 128
DN_V = 128
DN_CONV = 5
CHUNK = 64
N_SUB = 3
EPS = 1e-6
ROPE_THETA = 10000.0
LOG2E = math.log2(math.e)

LANES = 128
SUBLANES = 8
V7X_VMEM_LIMIT_BYTES = 56 * 1024 * 1024


def _tile(dim, pref):
    t = min(dim, pref)
    assert dim % t == 0, (dim, pref)
    return t


def _params(sem):
    return pltpu.CompilerParams(dimension_semantics=sem, vmem_limit_bytes=V7X_VMEM_LIMIT_BYTES)


def _dot(a, b):
    return jnp.dot(a, b, preferred_element_type=F32)


def _dot_nt(a, b):
    return lax.dot_general(a, b, (((1,), (1,)), ((), ())), preferred_element_type=F32)


def _dot_tn(a, b):
    return lax.dot_general(a, b, (((0,), (0,)), ((), ())), preferred_element_type=F32)


def _adaln_body(c_ref, w_ref, b_ref, o_ref):
    c = c_ref[...]
    h = (c * jax.nn.sigmoid(c)).astype(BF16)
    o_ref[...] = _dot(h, w_ref[...].astype(BF16)) + b_ref[...]


def _adaln(c, w_ada, b_ada):
    B, D = c.shape
    N = w_ada.shape[1]
    rows = max(SUBLANES, B)
    c_pad = jnp.zeros((rows, D), F32).at[:B].set(c)
    tn = _tile(N, 512)
    out = pl.pallas_call(
        _adaln_body,
        grid=(N // tn,),
        in_specs=[pl.BlockSpec((rows, D), lambda j: (0, 0)),
                  pl.BlockSpec((D, tn), lambda j: (0, j)),
                  pl.BlockSpec((1, tn), lambda j: (0, j))],
        out_specs=pl.BlockSpec((rows, tn), lambda j: (0, j)),
        out_shape=jax.ShapeDtypeStruct((rows, N), F32),
        compiler_params=_params(("parallel",)),
        name="adaln",
    )(c_pad, w_ada, b_ada.reshape(1, N))
    return out[:B]


def _post_body(*refs, has_mod):
    it = iter(refs)
    x_ref, g_ref, b_ref = next(it), next(it), next(it)
    if has_mod:
        sh_ref, sc_ref = next(it), next(it)
    xo_ref = next(it)
    if has_mod:
        h_ref = next(it)
    x = x_ref[...]
    mu = jnp.mean(x, axis=-1, keepdims=True)
    xc = x - mu
    var = jnp.mean(xc * xc, axis=-1, keepdims=True)
    y = xc * lax.rsqrt(var + EPS) * g_ref[...] + b_ref[...]
    xo_ref[...] = y
    if has_mod:
        h_ref[...] = (y * (1.0 + sc_ref[0]) + sh_ref[0]).astype(BF16)


def _post(x, g, b, shift, scale, *, seq):
    M, D = x.shape
    has_mod = shift is not None
    tm = _tile(seq, 256)
    per_b = seq // tm
    row = pl.BlockSpec((tm, D), lambda i: (i, 0))
    vec = pl.BlockSpec((1, D), lambda i: (0, 0))
    bvec = pl.BlockSpec((1, 1, D), lambda i: (i // per_b, 0, 0))
    args = [x, g.reshape(1, D), b.reshape(1, D)]
    specs = [row, vec, vec]
    if has_mod:
        args += [shift.reshape(-1, 1, D), scale.reshape(-1, 1, D)]
        specs += [bvec, bvec]
    out_shape = [jax.ShapeDtypeStruct((M, D), F32)]
    out_specs = [row]
    if has_mod:
        out_shape.append(jax.ShapeDtypeStruct((M, D), BF16))
        out_specs.append(row)
    outs = pl.pallas_call(
        functools.partial(_post_body, has_mod=has_mod),
        grid=(M // tm,),
        in_specs=specs,
        out_specs=out_specs,
        out_shape=out_shape,
        compiler_params=_params(("parallel",)),
        name="post_ln",
    )(*args)
    return outs if has_mod else (outs[0], None)


def _mm_body(*refs, nk, residual):
    a_ref, w_ref = refs[:2]
    refs = refs[2:]
    if residual is not None:
        (x_ref, gate_ref), refs = refs[:2], refs[2:]
    o_ref = refs[0]

    def finish(acc):
        if residual is not None:
            alpha, r = residual
            acc = alpha * x_ref[...] + (r * gate_ref[0]) * acc
        o_ref[...] = acc.astype(o_ref.dtype)

    if nk == 1:
        finish(_dot(a_ref[...], w_ref[...]))
        return
    acc_ref = refs[1]
    k = pl.program_id(2)

    @pl.when(k == 0)
    def _():
        acc_ref[...] = jnp.zeros_like(acc_ref)

    acc_ref[...] += _dot(a_ref[...], w_ref[...])

    @pl.when(k == nk - 1)
    def _():
        finish(acc_ref[...])


def _matmul(a, w, out_dtype, *, tm=1024, tn=1024, tk=4096, x=None, gate=None, seq=None, residual=None):
    M, K = a.shape
    N = w.shape[1]
    tm, tn, tk = _tile(M if residual is None else seq, tm), _tile(N, tn), _tile(K, tk)
    nk = K // tk
    args = [a, w]
    in_specs = [pl.BlockSpec((tm, tk), lambda i, j, k: (i, k)),
                pl.BlockSpec((tk, tn), lambda i, j, k: (k, j))]
    if residual is not None:
        per_b = seq // tm
        args += [x, gate.reshape(-1, 1, N)]
        in_specs += [pl.BlockSpec((tm, tn), lambda i, j, k: (i, j)),
                     pl.BlockSpec((1, 1, tn), lambda i, j, k: (i // per_b, 0, j))]
    return pl.pallas_call(
        functools.partial(_mm_body, nk=nk, residual=residual),
        grid=(M // tm, N // tn, nk),
        in_specs=in_specs,
        out_specs=pl.BlockSpec((tm, tn), lambda i, j, k: (i, j)),
        out_shape=jax.ShapeDtypeStruct((M, N), out_dtype),
        scratch_shapes=[pltpu.VMEM((tm, tn), F32)] if nk > 1 else [],
        compiler_params=_params(("parallel", "parallel", "arbitrary")),
        name="matmul",
    )(*args)


def _gateup_body(a_ref, wg_ref, wu_ref, o_ref, wgb_ref, wub_ref):
    @pl.when(pl.program_id(1) == 0)
    def _():
        wgb_ref[...] = wg_ref[0].astype(BF16)
        wub_ref[...] = wu_ref[0].astype(BF16)

    a = a_ref[...]
    g = _dot(a, wgb_ref[...])
    u = _dot(a, wub_ref[...])
    o_ref[...] = (g * jax.nn.sigmoid(g) * u).astype(o_ref.dtype)


def _gateup(a, wg, wu, layer):
    M, K = a.shape
    N = wg.shape[2]
    tm, tn = _tile(M, 512), _tile(N, 512)
    return pl.pallas_call(
        _gateup_body,
        grid=(N // tn, M // tm),
        in_specs=[pl.BlockSpec((tm, K), lambda j, i: (i, 0)),
                  pl.BlockSpec((1, K, tn), lambda j, i: (layer, 0, j)),
                  pl.BlockSpec((1, K, tn), lambda j, i: (layer, 0, j))],
        out_specs=pl.BlockSpec((tm, tn), lambda j, i: (i, j)),
        out_shape=jax.ShapeDtypeStruct((M, N), BF16),
        scratch_shapes=[pltpu.VMEM((K, tn), BF16), pltpu.VMEM((K, tn), BF16)],
        compiler_params=_params(("parallel", "arbitrary")),
        name="ffn_gate_up",
    )(a, wg, wu)


def _merge_body(oa_ref, ob_ref, wa_ref, wb_ref, ga_ref, gb_ref, o_ref):
    ya = _dot(oa_ref[...], wa_ref[...])
    yb = _dot(ob_ref[...], wb_ref[...])
    o_ref[...] = (jax.nn.sigmoid(ga_ref[...]) * ya + jax.nn.sigmoid(gb_ref[...]) * yb).astype(o_ref.dtype)


def _merge(oa, ob, wa, wb, p2, d_model, gate_col0):
    M, Ka = oa.shape
    Kb = ob.shape[1]
    tm, tn = _tile(M, 1024), _tile(d_model, 512)
    ja = gate_col0 // tn
    jb = (gate_col0 + d_model) // tn
    return pl.pallas_call(
        _merge_body,
        grid=(M // tm, d_model // tn),
        in_specs=[pl.BlockSpec((tm, Ka), lambda i, j: (i, 0)),
                  pl.BlockSpec((tm, Kb), lambda i, j: (i, 0)),
                  pl.BlockSpec((Ka, tn), lambda i, j: (0, j)),
                  pl.BlockSpec((Kb, tn), lambda i, j: (0, j)),
                  pl.BlockSpec((tm, tn), lambda i, j: (i, ja + j)),
                  pl.BlockSpec((tm, tn), lambda i, j: (i, jb + j))],
        out_specs=pl.BlockSpec((tm, tn), lambda i, j: (i, j)),
        out_shape=jax.ShapeDtypeStruct((M, d_model), BF16),
        compiler_params=_params(("parallel", "parallel")),
        name="branch_merge",
    )(oa, ob, wa, wb, p2, p2)


def _rms(x, g):
    return x * lax.rsqrt(jnp.mean(x * x, axis=-1, keepdims=True) + EPS) * g


def _mla_prep_body(ql_ref, kvl_ref, kr_ref, tab_ref, gq_ref, gkv_ref, wq_ref, wkv_ref,
                   q_ref, k_ref, v_ref, *, heads, q_scale):
    qn = _rms(ql_ref[...], gq_ref[...]).astype(BF16)
    kvn = _rms(kvl_ref[...], gkv_ref[...]).astype(BF16)
    tab = tab_ref[...]
    tab_sin = pltpu.roll(tab, 64, 1)
    x = kr_ref[...]
    lane = lax.broadcasted_iota(jnp.int32, x.shape, 1)
    x_rot = jnp.where(lane < ROPE // 2, -pltpu.roll(x, LANES - ROPE // 2, 1), pltpu.roll(x, ROPE // 2, 1))
    k_rope_t = (x * tab + x_rot * tab_sin).T[:ROPE].astype(BF16)
    for h in range(heads):
        r = _dot(qn, wq_ref[h])
        p = r[:, NOPE:] * tab
        roped = p + pltpu.roll(p, 64, 1)
        q_ref[0, h, :, :NOPE] = (r[:, :NOPE] * q_scale).astype(BF16)
        q_ref[0, h, :, NOPE:] = (roped[:, :ROPE] * q_scale).astype(BF16)
        kv = _dot(kvn, wkv_ref[h])
        k_ref[0, h, :NOPE, :] = kv[:, :NOPE].T.astype(BF16)
        k_ref[0, h, NOPE:, :] = k_rope_t
        v_ref[0, h] = kv[:, NOPE:].astype(BF16)


def _mla_prep(p1, tab, gq, gkv, wq, wkv, *, batch, seq):
    heads, q_rank, _ = wq.shape
    kv_rank = wkv.shape[1]
    ts = _tile(seq, 512)
    ns = seq // ts
    row = lambda b, i: b * ns + i
    return pl.pallas_call(
        functools.partial(_mla_prep_body, heads=heads, q_scale=MLA_QK ** -0.5 * LOG2E),
        grid=(batch, ns),
        in_specs=[pl.BlockSpec((ts, q_rank), lambda b, i: (row(b, i), 0)),
                  pl.BlockSpec((ts, kv_rank), lambda b, i: (row(b, i), q_rank // kv_rank)),
                  pl.BlockSpec((ts, LANES), lambda b, i: (row(b, i), (q_rank + kv_rank) // LANES)),
                  pl.BlockSpec((ts, LANES), lambda b, i: (row(b, i), 0)),
                  pl.BlockSpec((1, q_rank), lambda b, i: (0, 0)),
                  pl.BlockSpec((1, kv_rank), lambda b, i: (0, 0)),
                  pl.BlockSpec((heads, q_rank, 2 * LANES), lambda b, i: (0, 0, 0)),
                  pl.BlockSpec((heads, kv_rank, 2 * LANES), lambda b, i: (0, 0, 0))],
        out_specs=[pl.BlockSpec((1, heads, ts, MLA_QK), lambda b, i: (b, 0, i, 0)),
                   pl.BlockSpec((1, heads, MLA_QK, ts), lambda b, i: (b, 0, 0, i)),
                   pl.BlockSpec((1, heads, ts, MLA_V), lambda b, i: (b, 0, i, 0))],
        out_shape=[jax.ShapeDtypeStruct((batch, heads, seq, MLA_QK), BF16),
                   jax.ShapeDtypeStruct((batch, heads, MLA_QK, seq), BF16),
                   jax.ShapeDtypeStruct((batch, heads, seq, MLA_V), BF16)],
        compiler_params=_params(("parallel", "parallel")),
        name="mla_prep",
    )(p1, p1, p1, tab, gq.reshape(1, -1), gkv.reshape(1, -1), wq, wkv)


def _flash_body(q_ref, kt_ref, v_ref, o_ref, s_ref, m_ref, l_ref, acc_ref, *, tk, nk):
    m_ref[...] = jnp.full_like(m_ref, -jnp.inf)
    l_ref[...] = jnp.zeros_like(l_ref)
    acc_ref[...] = jnp.zeros_like(acc_ref)
    q = q_ref[0, 0]

    def key_rows(j):
        return pl.ds(pl.multiple_of(j * tk, tk), tk)

    def scores(j):
        return _dot(q, kt_ref[0, 0, :, key_rows(j)])

    s_ref[0] = scores(0)

    def block_pair(jj, carry):
        for u in range(2):
            j = 2 * jj + u
            s = s_ref[u]
            s_ref[1 - u] = scores(j + 1 if u == 0 else jnp.minimum(j + 1, nk - 1))
            m_prev = m_ref[...]
            m_new = jnp.maximum(m_prev, jnp.max(s, axis=1, keepdims=True))
            alpha = jnp.exp2(m_prev - m_new)
            p = jnp.exp2(s - m_new[:, :1])
            l_ref[...] = alpha * l_ref[...] + jnp.sum(p, axis=1, keepdims=True)
            acc_ref[...] = alpha * acc_ref[...] + _dot(p.astype(BF16), v_ref[0, 0, key_rows(j), :])
            m_ref[...] = m_new
        return carry

    lax.fori_loop(0, nk // 2, block_pair, 0)
    o_ref[0] = (acc_ref[...] / l_ref[...]).astype(o_ref.dtype)


def _flash(q, k, v):
    B, H, S, _ = q.shape
    tq, tk = _tile(S, 512), _tile(S, 1024)
    if (S // tk) % 2:
        tk //= 2
    return pl.pallas_call(
        functools.partial(_flash_body, tk=tk, nk=S // tk),
        grid=(B, H, S // tq),
        in_specs=[pl.BlockSpec((1, 1, tq, MLA_QK), lambda b, h, i: (b, h, i, 0)),
                  pl.BlockSpec((1, 1, MLA_QK, S), lambda b, h, i: (b, h, 0, 0)),
                  pl.BlockSpec((1, 1, S, MLA_V), lambda b, h, i: (b, h, 0, 0))],
        out_specs=pl.BlockSpec((1, tq, MLA_V), lambda b, h, i: (b, i, h)),
        out_shape=jax.ShapeDtypeStruct((B, S, H * MLA_V), BF16),
        scratch_shapes=[pltpu.VMEM((2, tq, tk), F32), pltpu.VMEM((tq, LANES), F32),
                        pltpu.VMEM((tq, LANES), F32), pltpu.VMEM((tq, MLA_V), F32)],
        compiler_params=_params(("parallel", "parallel", "arbitrary")),
        name="mla_flash",
    )(q, k, v)


def _dn_conv_body(cur_ref, prev_ref, next_ref, w_ref, o_ref, buf_ref, *, ts, ns, qk_blocks):
    i = pl.program_id(1)
    c = pl.program_id(2)
    pad = DN_CONV // 2
    buf_ref[0:SUBLANES] = jnp.where(i == 0, 0.0, prev_ref[...])
    buf_ref[SUBLANES:SUBLANES + ts] = cur_ref[...]
    buf_ref[SUBLANES + ts:] = jnp.where(i == ns - 1, 0.0, next_ref[...])
    w = w_ref[...]
    acc = w[0:1] * buf_ref[SUBLANES - pad:SUBLANES - pad + ts]
    for t in range(1, DN_CONV):
        acc = acc + w[t:t + 1] * buf_ref[SUBLANES - pad + t:SUBLANES - pad + t + ts]
    y = acc * jax.nn.sigmoid(acc)
    kind = c // qk_blocks
    for g in range(y.shape[1] // DN_K):
        yg = y[:, g * DN_K:(g + 1) * DN_K]
        inv = lax.rsqrt(jnp.sum(yg * yg, axis=-1, keepdims=True) + EPS)
        inv = jnp.where(kind < 2, inv, 1.0) * jnp.where(kind == 0, DN_K ** -0.5, 1.0)
        o_ref[:, g * DN_K:(g + 1) * DN_K] = (yg * inv).astype(o_ref.dtype)


def _dn_conv(p2, conv_w, *, batch, seq, heads):
    width = 3 * heads * DN_K
    ts = _tile(seq, 512)
    ns = seq // ts
    cb = 4 * DN_K
    qk_blocks = heads * DN_K // cb
    r8 = ts // SUBLANES
    last8 = batch * seq // SUBLANES - 1
    return pl.pallas_call(
        functools.partial(_dn_conv_body, ts=ts, ns=ns, qk_blocks=qk_blocks),
        grid=(batch, ns, width // cb),
        in_specs=[pl.BlockSpec((ts, cb), lambda b, i, c: (b * ns + i, c)),
                  pl.BlockSpec((SUBLANES, cb), lambda b, i, c: (jnp.maximum((b * ns + i) * r8 - 1, 0), c)),
                  pl.BlockSpec((SUBLANES, cb), lambda b, i, c: (jnp.minimum((b * ns + i + 1) * r8, last8), c)),
                  pl.BlockSpec((DN_CONV, cb), lambda b, i, c: (0, c))],
        out_specs=pl.BlockSpec((ts, cb), lambda b, i, c: (b * ns + i, c)),
        out_shape=jax.ShapeDtypeStruct((batch * seq, width), BF16),
        scratch_shapes=[pltpu.VMEM((ts + 2 * SUBLANES, cb), F32)],
        compiler_params=_params(("parallel", "parallel", "parallel")),
        name="dn_conv",
    )(p2, p2, p2, conv_w)


def _dn_gate_body(x_ref, alog_ref, dtb_ref, o_ref, *, heads):
    x = x_ref[...]
    z = x + dtb_ref[...]
    softplus = jnp.maximum(z, 0.0) + jnp.log1p(jnp.exp(-jnp.abs(z)))
    g = -jnp.exp(alog_ref[...]) * softplus
    lane = lax.broadcasted_iota(jnp.int32, x.shape, 1)
    zt = jnp.where(lane < LANES - 2 * heads, g, jax.nn.sigmoid(x)).T
    o_ref[0, 2 * heads:] = zt[LANES - 2 * heads:]
    pos = lax.broadcasted_iota(jnp.int32, (2 * heads, LANES), 1) % CHUNK
    fwd = lax.broadcasted_iota(jnp.int32, (2 * heads, LANES), 0) < heads
    for grp in range(zt.shape[1] // LANES):
        cols = slice(grp * LANES, (grp + 1) * LANES)
        gt = zt[LANES - 4 * heads:LANES - 2 * heads, cols]
        pre, suf = gt, gt
        s = 1
        while s < CHUNK:
            pre = pre + jnp.where(pos >= s, pltpu.roll(pre, s, 1), 0.0)
            suf = suf + jnp.where(pos < CHUNK - s, pltpu.roll(suf, LANES - s, 1), 0.0)
            s *= 2
        o_ref[0, :2 * heads, cols] = jnp.where(fwd, pre, suf)


def _dn_gates(p1, a_log, dt_bias, *, batch, seq, heads, col_block):
    ts = _tile(seq, 512)
    ns = seq // ts
    lo, hi = LANES - 4 * heads, LANES - 2 * heads
    alog = jnp.zeros((1, LANES), F32).at[0, lo:hi].set(a_log.reshape(-1))
    dtb = jnp.zeros((1, LANES), F32).at[0, lo:hi].set(dt_bias.reshape(-1))
    return pl.pallas_call(
        functools.partial(_dn_gate_body, heads=heads),
        grid=(batch, ns),
        in_specs=[pl.BlockSpec((ts, LANES), lambda b, i: (b * ns + i, col_block)),
                  pl.BlockSpec((1, LANES), lambda b, i: (0, 0)),
                  pl.BlockSpec((1, LANES), lambda b, i: (0, 0))],
        out_specs=pl.BlockSpec((1, 4 * heads, ts), lambda b, i: (b, 0, i)),
        out_shape=jax.ShapeDtypeStruct((batch, 4 * heads, seq), F32),
        compiler_params=_params(("parallel", "parallel")),
        name="dn_gates",
    )(p1, alog, dtb)


def _split_dot(a, b):
    ah, bh = a.astype(BF16), b.astype(BF16)
    al = (a - ah.astype(F32)).astype(BF16)
    bl = (b - bh.astype(F32)).astype(BF16)
    return _dot(ah, bh) + _dot(ah, bl) + _dot(al, bh)


def _unit_triangular_inverses(ms, eye):
    a = [eye + m for m in ms]
    ab = [x.astype(BF16) for x in a]
    t = [eye - m for m in ms]
    for _ in range(int(math.log2(CHUNK)) - 2):
        r = [2.0 * eye - _dot(x, y.astype(BF16)) for x, y in zip(ab, t)]
        t = [_dot(x.astype(BF16), y.astype(BF16)) for x, y in zip(t, r)]
    r = [eye - _split_dot(x, y) for x, y in zip(a, t)]
    return [x + _dot(x.astype(BF16), y.astype(BF16)) for x, y in zip(t, r)]


def _dn_local_body(q_ref, k_ref, v_ref, grow_ref, glast_ref,
                   u_ref, w_ref, qd_ref, kd_ref, qk_ref, *, chunks):
    ii = lax.broadcasted_iota(jnp.int32, (CHUNK, CHUNK), 0)
    jj = lax.broadcasted_iota(jnp.int32, (CHUNK, CHUNK), 1)
    eye = (ii == jj).astype(F32)
    rows = [slice(c * CHUNK, (c + 1) * CHUNK) for c in range(chunks)]
    q = [q_ref[r, :] for r in rows]
    k = [k_ref[r, :] for r in rows]
    v = [v_ref[r, :] for r in rows]
    qk_kk = [_dot_nt(jnp.concatenate([x, y], axis=0), y) for x, y in zip(q, k)]

    chains = [(c, d) for c in range(chunks) for d in range(2)]
    g_row, b_row, g_col, g_last, decay, ms = [], [], [], [], [], []
    for c, d in chains:
        incl = (jj <= ii) if d == 0 else (jj >= ii)
        strict = (jj < ii) if d == 0 else (jj > ii)
        r = jnp.broadcast_to(grow_ref[0, d, 0, c:c + 1, :], (LANES, LANES))
        rt = r.T
        gc, bc = rt[:CHUNK], rt[CHUNK:]
        gr = r[:CHUNK, :CHUNK]
        dec = jnp.exp(jnp.where(incl, gc[:, :CHUNK] - gr, -jnp.inf))
        ms.append(jnp.where(strict, qk_kk[c][CHUNK:] * bc[:, :CHUNK] * dec, 0.0))
        g_row.append(gr)
        b_row.append(r[:CHUNK, CHUNK:])
        g_col.append(gc)
        g_last.append(jnp.broadcast_to(glast_ref[0, d, 0, c:c + 1, :], (CHUNK, LANES)))
        decay.append(dec)

    t_beta = [t * b for t, b in zip(_unit_triangular_inverses(ms, eye), b_row)]
    us = [_dot(t.astype(BF16), v[c]) for t, (c, d) in zip(t_beta, chains)]
    ws = [_dot((t * jnp.exp(g)).astype(BF16), k[c]) for t, g, (c, d) in zip(t_beta, g_row, chains)]
    for i, (c, d) in enumerate(chains):
        u_ref[0, d, 0, rows[c], :] = us[i]
        w_ref[0, d, 0, rows[c], :] = ws[i].astype(BF16)
        qk_ref[0, d, 0, rows[c], :] = (qk_kk[c][:CHUNK] * decay[i]).astype(BF16)
        qd_ref[0, d, 0, rows[c], :] = (q[c].astype(F32) * jnp.exp(g_col[i])).astype(BF16)
        kd_ref[0, d, 0, rows[c], :] = (k[c].astype(F32) * jnp.exp(g_last[i] - g_col[i])).astype(BF16)


def _dn_local(qkv, grow, glast, *, batch, seq, heads):
    nc = seq // CHUNK
    cbk = _tile(nc, 8)
    L = cbk * CHUNK
    nb = nc // cbk
    tok = lambda off: pl.BlockSpec((L, DN_K), lambda b, h, t: (b * nb + t, off + h))
    gate = pl.BlockSpec((1, 2, 1, cbk, LANES), lambda b, h, t: (b, 0, h, t, 0))
    out = lambda width: pl.BlockSpec((1, 2, 1, L, width), lambda b, h, t: (b, 0, h, t, 0))
    shp = lambda width, dt: jax.ShapeDtypeStruct((batch, 2, heads, seq, width), dt)
    return pl.pallas_call(
        functools.partial(_dn_local_body, chunks=cbk),
        grid=(batch, heads, nb),
        in_specs=[tok(0), tok(heads), tok(2 * heads), gate, gate],
        out_specs=[out(DN_V), out(DN_K), out(DN_K), out(DN_K), out(CHUNK)],
        out_shape=[shp(DN_V, F32), shp(DN_K, BF16), shp(DN_K, BF16), shp(DN_K, BF16), shp(CHUNK, BF16)],
        compiler_params=_params(("parallel", "parallel", "parallel")),
        name="dn_local",
    )(qkv, qkv, qkv, grow, glast)


def _dn_scan_body(*refs, chunks, hb):
    ins, (of_ref, ob_ref, st_ref) = refs[:12], refs[12:]
    t = pl.program_id(2)

    @pl.when(t == 0)
    def _():
        st_ref[...] = jnp.zeros_like(st_ref)

    chains = [(d, j) for d in range(2) for j in range(hb)]
    states = [st_ref[d, j] for d, j in chains]
    for ci in range(chunks):
        blk = []
        for d, j in chains:
            u_ref, w_ref, qd_ref, kd_ref, qk_ref, gl_ref = ins[6 * d:6 * d + 6]
            c = ci if d == 0 else chunks - 1 - ci
            rows = slice(c * CHUNK, (c + 1) * CHUNK)
            blk.append((u_ref[0, 0, j, rows, :], w_ref[0, 0, j, rows, :], qd_ref[0, 0, j, rows, :],
                        kd_ref[0, 0, j, rows, :], qk_ref[0, 0, j, rows, :],
                        jnp.exp(gl_ref[0, 0, j, c:c + 1, :]), rows))
        sb = [s.astype(BF16) for s in states]
        w_s = [_dot(b[1], s) for b, s in zip(blk, sb)]
        q_s = [_dot(b[2], s) for b, s in zip(blk, sb)]
        vb = [(b[0] - x).astype(BF16) for b, x in zip(blk, w_s)]
        outs = [x + _dot(b[4], y) for b, x, y in zip(blk, q_s, vb)]
        states = [s * b[5] + _dot_tn(b[3], y) for b, s, y in zip(blk, states, vb)]
        for (d, j), b, o in zip(chains, blk, outs):
            o_ref = of_ref if d == 0 else ob_ref
            o_ref[b[6], j * DN_V:(j + 1) * DN_V] = o
    for (d, j), s in zip(chains, states):
        st_ref[d, j] = s


def _dn_scan(u, w, qd, kd, qk, glast, *, batch, seq, heads):
    nc = seq // CHUNK
    cbk = _tile(nc, 8)
    L = cbk * CHUNK
    nb = nc // cbk
    hb = _tile(heads, 4)

    def spec(d, rows, width):
        if d == 0:
            return pl.BlockSpec((1, 1, hb, rows, width), lambda b, g, t: (b, 0, g, t, 0))
        return pl.BlockSpec((1, 1, hb, rows, width), lambda b, g, t: (b, 1, g, nb - 1 - t, 0))

    in_specs, args = [], []
    for d in range(2):
        in_specs += [spec(d, L, DN_V), spec(d, L, DN_K), spec(d, L, DN_K), spec(d, L, DN_K),
                     spec(d, L, CHUNK), spec(d, cbk, LANES)]
        args += [u, w, qd, kd, qk, glast]
    out_shape = jax.ShapeDtypeStruct((batch * seq, heads * DN_V), F32)
    return pl.pallas_call(
        functools.partial(_dn_scan_body, chunks=cbk, hb=hb),
        grid=(batch, heads // hb, nb),
        in_specs=in_specs,
        out_specs=[pl.BlockSpec((L, hb * DN_V), lambda b, g, t: (b * nb + t, g)),
                   pl.BlockSpec((L, hb * DN_V), lambda b, g, t: (b * nb + nb - 1 - t, g))],
        out_shape=[out_shape, out_shape],
        scratch_shapes=[pltpu.VMEM((2, hb, DN_K, DN_V), F32)],
        compiler_params=_params(("parallel", "parallel", "arbitrary")),
        name="dn_scan",
    )(*args)


def _dn_out_body(of_ref, ob_ref, z_ref, g_ref, o_ref):
    o = of_ref[...] + ob_ref[...]
    z = z_ref[...]
    g = g_ref[...]
    for h in range(o.shape[1] // DN_V):
        sl = slice(h * DN_V, (h + 1) * DN_V)
        oh, zh = o[:, sl], z[:, sl]
        y = oh * lax.rsqrt(jnp.mean(oh * oh, axis=-1, keepdims=True) + EPS) * g
        o_ref[:, sl] = (y * (zh * jax.nn.sigmoid(zh))).astype(o_ref.dtype)


def _dn_out(o_f, o_b, p2, norm_g, *, z_col0):
    M, W = o_f.shape
    tm, tw = _tile(M, 512), _tile(W, 512)
    zj = z_col0 // tw
    return pl.pallas_call(
        _dn_out_body,
        grid=(M // tm, W // tw),
        in_specs=[pl.BlockSpec((tm, tw), lambda i, j: (i, j)),
                  pl.BlockSpec((tm, tw), lambda i, j: (i, j)),
                  pl.BlockSpec((tm, tw), lambda i, j: (i, zj + j)),
                  pl.BlockSpec((1, DN_V), lambda i, j: (0, 0))],
        out_specs=pl.BlockSpec((tm, tw), lambda i, j: (i, j)),
        out_shape=jax.ShapeDtypeStruct((M, W), BF16),
        compiler_params=_params(("parallel", "parallel")),
        name="dn_out",
    )(o_f, o_b, p2, norm_g.reshape(1, DN_V))


def _rope_table(positions):
    half = ROPE // 2
    inv_freq = ROPE_THETA ** (-jnp.arange(half, dtype=F32) / half)
    ang = positions.astype(F32)[..., None] * inv_freq
    cos, sin = jnp.cos(ang), jnp.sin(ang)
    return jnp.concatenate([cos, cos, sin, sin], axis=-1).reshape(-1, 4 * half)


def _mixer(u, tab, w_in, q_norm_g, kv_norm_g, w_uq, w_ukv, w_branch_a, conv_w, a_log, dt_bias,
           o_norm_g, w_branch_b, w_out, *, batch, seq, x, gate, alpha):
    d_model = u.shape[1]
    q_rank, kv_rank = q_norm_g.shape[0], kv_norm_g.shape[0]
    mla_heads = w_uq.shape[1] // MLA_QK
    dn_heads = a_log.shape[1]
    dn_qk_w, dn_v_w = dn_heads * DN_K, dn_heads * DN_V
    c_qkv = q_rank + kv_rank + ROPE
    c_z = c_qkv + 2 * dn_qk_w + dn_v_w
    c_a = c_z + dn_v_w
    c_gate = c_a + 4 * dn_heads
    assert ROPE + 4 * dn_heads == LANES and c_qkv - ROPE == (q_rank + kv_rank)
    w1 = jnp.concatenate([w_in[:, :c_qkv], w_in[:, c_a:c_gate]], axis=1).astype(BF16)
    w2 = jnp.concatenate([w_in[:, c_qkv:c_a], w_in[:, c_gate:]], axis=1).astype(BF16)
    p1 = _matmul(u, w1, F32, tm=512, tn=w1.shape[1])
    p2 = _matmul(u, w2, F32)

    wq = w_uq.reshape(q_rank, mla_heads, MLA_QK)
    r1, r2 = wq[..., NOPE:NOPE + ROPE // 2], wq[..., NOPE + ROPE // 2:]
    wq = jnp.concatenate([wq, -r2, r1], axis=-1).transpose(1, 0, 2).astype(BF16)
    wkv = w_ukv.reshape(kv_rank, mla_heads, NOPE + MLA_V).transpose(1, 0, 2).astype(BF16)
    q, k, v = _mla_prep(p1, tab, q_norm_g, kv_norm_g, wq, wkv, batch=batch, seq=seq)
    o_a = _flash(q, k, v).reshape(batch * seq, mla_heads * MLA_V)

    qkv = _dn_conv(p2, conv_w, batch=batch, seq=seq, heads=dn_heads)
    gates = _dn_gates(p1, a_log, dt_bias, batch=batch, seq=seq, heads=dn_heads,
                      col_block=(q_rank + kv_rank) // LANES)
    nc = seq // CHUNK
    gates = gates.reshape(batch, 2, 2, dn_heads, nc, CHUNK)
    grow = gates.transpose(0, 2, 3, 4, 1, 5).reshape(batch, 2, dn_heads, nc, 2 * CHUNK)
    g_cum = gates[:, 0]
    glast = jnp.stack([g_cum[:, 0, :, :, CHUNK - 1], g_cum[:, 1, :, :, 0]], axis=1)
    glast = jnp.broadcast_to(glast[..., None], glast.shape + (LANES,))
    u_, w_, qd, kd, qk = _dn_local(qkv, grow, glast, batch=batch, seq=seq, heads=dn_heads)
    o_f, o_b = _dn_scan(u_, w_, qd, kd, qk, glast, batch=batch, seq=seq, heads=dn_heads)
    o_bn = _dn_out(o_f, o_b, p2, o_norm_g, z_col0=2 * dn_qk_w + dn_v_w)

    merged = _merge(o_a, o_bn, w_branch_a.astype(BF16), w_branch_b.astype(BF16), p2, d_model,
                    gate_col0=2 * dn_qk_w + 2 * dn_v_w)
    return _matmul(merged, w_out.astype(BF16), F32, tn=512, x=x, gate=gate, seq=seq, residual=(alpha, 1.0))


def _ffn(h, w_gate, w_up, w_down, layer, *, x, gate, seq, alpha):
    a = _gateup(h, w_gate, w_up, layer)
    return _matmul(a, w_down[layer].astype(BF16), F32, tm=512, tn=512, tk=w_down.shape[1],
                   x=x, gate=gate, seq=seq, residual=(alpha, 0.5))


def kernel(x, c, positions, ln_in_g, ln_in_b, w_ada, b_ada, ada_table, ffn1_w_gate, ffn1_w_up, ffn1_w_down, w_in, mla_q_norm_g, mla_kv_norm_g, mla_w_uq, mla_w_ukv, w_branch_a, dn_conv_w, dn_a_log, dn_dt_bias, dn_norm_g, w_branch_b, w_out, ffn2_w_gate, ffn2_w_up, ffn2_w_down, post_ln_g, post_ln_b):
    B, S, D = x.shape
    depth = ada_table.shape[0]
    alpha = (2.0 * depth) ** 0.25
    cond = _adaln(c, w_ada, b_ada).reshape(B, N_SUB, 3, D)
    mod = cond[None] + ada_table[:, None]
    tab = _rope_table(positions)
    post = functools.partial(_post, seq=S)

    xf, h = post(x.reshape(B * S, D), ln_in_g, ln_in_b, mod[0, :, 0, 0], mod[0, :, 0, 1])
    for l in range(depth):
        xs = _ffn(h, ffn1_w_gate, ffn1_w_up, ffn1_w_down, l, x=xf, gate=mod[l, :, 0, 2], seq=S, alpha=alpha)
        xf, h = post(xs, post_ln_g[l, 0], post_ln_b[l, 0], mod[l, :, 1, 0], mod[l, :, 1, 1])
        xs = _mixer(h, tab, w_in[l], mla_q_norm_g[l], mla_kv_norm_g[l], mla_w_uq[l], mla_w_ukv[l],
                    w_branch_a[l], dn_conv_w[l], dn_a_log[l], dn_dt_bias[l], dn_norm_g[l],
                    w_branch_b[l], w_out[l], batch=B, seq=S, x=xf, gate=mod[l, :, 1, 2], alpha=alpha)
        xf, h = post(xs, post_ln_g[l, 1], post_ln_b[l, 1], mod[l, :, 2, 0], mod[l, :, 2, 1])
        xs = _ffn(h, ffn2_w_gate, ffn2_w_up, ffn2_w_down, l, x=xf, gate=mod[l, :, 2, 2], seq=S, alpha=alpha)
        last = l == depth - 1
        xf, h = post(xs, post_ln_g[l, 2], post_ln_b[l, 2],
                     None if last else mod[l + 1, :, 0, 0], None if last else mod[l + 1, :, 0, 1])
    return xf.reshape(B, S, D)
```

```python
import functools
import math

import jax
import jax.numpy as jnp
from jax import lax
from jax.experimental import pallas as pl
from jax.experimental.pallas import tpu as pltpu

F32 = jnp.float32
BF16 = jnp.bfloat16

NOPE = 128
ROPE = 64
MLA_V = 128
MLA_QK = NOPE + ROPE
DN_K = 128
DN_V = 128
DN_CONV = 5
CHUNK = 64
N_SUB = 3
EPS = 1e-6
ROPE_THETA = 10000.0
LOG2E = math.log2(math.e)

LANES = 128
SUBLANES = 8
V7X_VMEM_LIMIT_BYTES = 56 * 1024 * 1024


def _tile(dim, pref):
    t = min(dim, pref)
    assert dim % t == 0, (dim, pref)
    return t


def _params(sem):
    return pltpu.CompilerParams(dimension_semantics=sem, vmem_limit_bytes=V7X_VMEM_LIMIT_BYTES)


def _dot(a, b):
    return jnp.dot(a, b, preferred_element_type=F32)


def _dot_nt(a, b):
    return lax.dot_general(a, b, (((1,), (1,)), ((), ())), preferred_element_type=F32)


def _dot_tn(a, b):
    return lax.dot_general(a, b, (((0,), (0,)), ((), ())), preferred_element_type=F32)


def _adaln_body(c_ref, w_ref, b_ref, o_ref):
    c = c_ref[...]
    h = (c * jax.nn.sigmoid(c)).astype(BF16)
    o_ref[...] = _dot(h, w_ref[...].astype(BF16)) + b_ref[...]


def _adaln(c, w_ada, b_ada):
    B, D = c.shape
    N = w_ada.shape[1]
    rows = max(SUBLANES, B)
    c_pad = jnp.zeros((rows, D), F32).at[:B].set(c)
    tn = _tile(N, 512)
    out = pl.pallas_call(
        _adaln_body,
        grid=(N // tn,),
        in_specs=[pl.BlockSpec((rows, D), lambda j: (0, 0)),
                  pl.BlockSpec((D, tn), lambda j: (0, j)),
                  pl.BlockSpec((1, tn), lambda j: (0, j))],
        out_specs=pl.BlockSpec((rows, tn), lambda j: (0, j)),
        out_shape=jax.ShapeDtypeStruct((rows, N), F32),
        compiler_params=_params(("parallel",)),
        name="adaln",
    )(c_pad, w_ada, b_ada.reshape(1, N))
    return out[:B]


def _post_body(*refs, has_mod):
    it = iter(refs)
    x_ref, g_ref, b_ref = next(it), next(it), next(it)
    if has_mod:
        sh_ref, sc_ref = next(it), next(it)
    xo_ref = next(it)
    if has_mod:
        h_ref = next(it)
    x = x_ref[...]
    mu = jnp.mean(x, axis=-1, keepdims=True)
    xc = x - mu
    var = jnp.mean(xc * xc, axis=-1, keepdims=True)
    y = xc * lax.rsqrt(var + EPS) * g_ref[...] + b_ref[...]
    xo_ref[...] = y
    if has_mod:
        h_ref[...] = (y * (1.0 + sc_ref[0]) + sh_ref[0]).astype(BF16)


def _post(x, g, b, shift, scale, *, seq):
    M, D = x.shape
    has_mod = shift is not None
    tm = _tile(seq, 256)
    per_b = seq // tm
    row = pl.BlockSpec((tm, D), lambda i: (i, 0))
    vec = pl.BlockSpec((1, D), lambda i: (0, 0))
    bvec = pl.BlockSpec((1, 1, D), lambda i: (i // per_b, 0, 0))
    args = [x, g.reshape(1, D), b.reshape(1, D)]
    specs = [row, vec, vec]
    if has_mod:
        args += [shift.reshape(-1, 1, D), scale.reshape(-1, 1, D)]
        specs += [bvec, bvec]
    out_shape = [jax.ShapeDtypeStruct((M, D), F32)]
    out_specs = [row]
    if has_mod:
        out_shape.append(jax.ShapeDtypeStruct((M, D), BF16))
        out_specs.append(row)
    outs = pl.pallas_call(
        functools.partial(_post_body, has_mod=has_mod),
        grid=(M // tm,),
        in_specs=specs,
        out_specs=out_specs,
        out_shape=out_shape,
        compiler_params=_params(("parallel",)),
        name="post_ln",
    )(*args)
    return outs if has_mod else (outs[0], None)


def _mm_body(*refs, nk, residual):
    a_ref, w_ref = refs[:2]
    refs = refs[2:]
    if residual is not None:
        (x_ref, gate_ref), refs = refs[:2], refs[2:]
    o_ref = refs[0]

    def finish(acc):
        if residual is not None:
            alpha, r = residual
            acc = alpha * x_ref[...] + (r * gate_ref[0]) * acc
        o_ref[...] = acc.astype(o_ref.dtype)

    if nk == 1:
        finish(_dot(a_ref[...], w_ref[...]))
        return
    acc_ref = refs[1]
    k = pl.program_id(2)

    @pl.when(k == 0)
    def _():
        acc_ref[...] = jnp.zeros_like(acc_ref)

    acc_ref[...] += _dot(a_ref[...], w_ref[...])

    @pl.when(k == nk - 1)
    def _():
        finish(acc_ref[...])


def _matmul(a, w, out_dtype, *, tm=1024, tn=1024, tk=4096, x=None, gate=None, seq=None, residual=None):
    M, K = a.shape
    N = w.shape[1]
    tm, tn, tk = _tile(M if residual is None else seq, tm), _tile(N, tn), _tile(K, tk)
    nk = K // tk
    args = [a, w]
    in_specs = [pl.BlockSpec((tm, tk), lambda i, j, k: (i, k)),
                pl.BlockSpec((tk, tn), lambda i, j, k: (k, j))]
    if residual is not None:
        per_b = seq // tm
        args += [x, gate.reshape(-1, 1, N)]
        in_specs += [pl.BlockSpec((tm, tn), lambda i, j, k: (i, j)),
                     pl.BlockSpec((1, 1, tn), lambda i, j, k: (i // per_b, 0, j))]
    return pl.pallas_call(
        functools.partial(_mm_body, nk=nk, residual=residual),
        grid=(M // tm, N // tn, nk),
        in_specs=in_specs,
        out_specs=pl.BlockSpec((tm, tn), lambda i, j, k: (i, j)),
        out_shape=jax.ShapeDtypeStruct((M, N), out_dtype),
        scratch_shapes=[pltpu.VMEM((tm, tn), F32)] if nk > 1 else [],
        compiler_params=_params(("parallel", "parallel", "arbitrary")),
        name="matmul",
    )(*args)


def _gateup_body(a_ref, wg_ref, wu_ref, o_ref, wgb_ref, wub_ref):
    @pl.when(pl.program_id(1) == 0)
    def _():
        wgb_ref[...] = wg_ref[0].astype(BF16)
        wub_ref[...] = wu_ref[0].astype(BF16)

    a = a_ref[...]
    g = _dot(a, wgb_ref[...])
    u = _dot(a, wub_ref[...])
    o_ref[...] = (g * jax.nn.sigmoid(g) * u).astype(o_ref.dtype)


def _gateup(a, wg, wu, layer):
    M, K = a.shape
    N = wg.shape[2]
    tm, tn = _tile(M, 512), _tile(N, 512)
    return pl.pallas_call(
        _gateup_body,
        grid=(N // tn, M // tm),
        in_specs=[pl.BlockSpec((tm, K), lambda j, i: (i, 0)),
                  pl.BlockSpec((1, K, tn), lambda j, i: (layer, 0, j)),
                  pl.BlockSpec((1, K, tn), lambda j, i: (layer, 0, j))],
        out_specs=pl.BlockSpec((tm, tn), lambda j, i: (i, j)),
        out_shape=jax.ShapeDtypeStruct((M, N), BF16),
        scratch_shapes=[pltpu.VMEM((K, tn), BF16), pltpu.VMEM((K, tn), BF16)],
        compiler_params=_params(("parallel", "arbitrary")),
        name="ffn_gate_up",
    )(a, wg, wu)


def _merge_body(oa_ref, ob_ref, wa_ref, wb_ref, ga_ref, gb_ref, o_ref):
    ya = _dot(oa_ref[...], wa_ref[...])
    yb = _dot(ob_ref[...], wb_ref[...])
    o_ref[...] = (jax.nn.sigmoid(ga_ref[...]) * ya + jax.nn.sigmoid(gb_ref[...]) * yb).astype(o_ref.dtype)


def _merge(oa, ob, wa, wb, p2, d_model, gate_col0):
    M, Ka = oa.shape
    Kb = ob.shape[1]
    tm, tn = _tile(M, 1024), _tile(d_model, 512)
    ja = gate_col0 // tn
    jb = (gate_col0 + d_model) // tn
    return pl.pallas_call(
        _merge_body,
        grid=(M // tm, d_model // tn),
        in_specs=[pl.BlockSpec((tm, Ka), lambda i, j: (i, 0)),
                  pl.BlockSpec((tm, Kb), lambda i, j: (i, 0)),
                  pl.BlockSpec((Ka, tn), lambda i, j: (0, j)),
                  pl.BlockSpec((Kb, tn), lambda i, j: (0, j)),
                  pl.BlockSpec((tm, tn), lambda i, j: (i, ja + j)),
                  pl.BlockSpec((tm, tn), lambda i, j: (i, jb + j))],
        out_specs=pl.BlockSpec((tm, tn), lambda i, j: (i, j)),
        out_shape=jax.ShapeDtypeStruct((M, d_model), BF16),
        compiler_params=_params(("parallel", "parallel")),
        name="branch_merge",
    )(oa, ob, wa, wb, p2, p2)


def _rms(x, g):
    return x * lax.rsqrt(jnp.mean(x * x, axis=-1, keepdims=True) + EPS) * g


def _mla_prep_body(ql_ref, kvl_ref, kr_ref, tab_ref, gq_ref, gkv_ref, wq_ref, wkv_ref,
                   q_ref, k_ref, v_ref, *, heads, q_scale):
    qn = _rms(ql_ref[...], gq_ref[...]).astype(BF16)
    kvn = _rms(kvl_ref[...], gkv_ref[...]).astype(BF16)
    tab = tab_ref[...]
    tab_sin = pltpu.roll(tab, 64, 1)
    x = kr_ref[...]
    lane = lax.broadcasted_iota(jnp.int32, x.shape, 1)
    x_rot = jnp.where(lane < ROPE // 2, -pltpu.roll(x, LANES - ROPE // 2, 1), pltpu.roll(x, ROPE // 2, 1))
    k_rope_t = (x * tab + x_rot * tab_sin).T[:ROPE].astype(BF16)
    for h in range(heads):
        r = _dot(qn, wq_ref[h])
        p = r[:, NOPE:] * tab
        roped = p + pltpu.roll(p, 64, 1)
        q_ref[0, h, :, :NOPE] = (r[:, :NOPE] * q_scale).astype(BF16)
        q_ref[0, h, :, NOPE:] = (roped[:, :ROPE] * q_scale).astype(BF16)
        kv = _dot(kvn, wkv_ref[h])
        k_ref[0, h, :NOPE, :] = kv[:, :NOPE].T.astype(BF16)
        k_ref[0, h, NOPE:, :] = k_rope_t
        v_ref[0, h] = kv[:, NOPE:].astype(BF16)


def _mla_prep(p1, tab, gq, gkv, wq, wkv, *, batch, seq):
    heads, q_rank, _ = wq.shape
    kv_rank = wkv.shape[1]
    ts = _tile(seq, 512)
    ns = seq // ts
    row = lambda b, i: b * ns + i
    return pl.pallas_call(
        functools.partial(_mla_prep_body, heads=heads, q_scale=MLA_QK ** -0.5 * LOG2E),
        grid=(batch, ns),
        in_specs=[pl.BlockSpec((ts, q_rank), lambda b, i: (row(b, i), 0)),
                  pl.BlockSpec((ts, kv_rank), lambda b, i: (row(b, i), q_rank // kv_rank)),
                  pl.BlockSpec((ts, LANES), lambda b, i: (row(b, i), (q_rank + kv_rank) // LANES)),
                  pl.BlockSpec((ts, LANES), lambda b, i: (row(b, i), 0)),
                  pl.BlockSpec((1, q_rank), lambda b, i: (0, 0)),
                  pl.BlockSpec((1, kv_rank), lambda b, i: (0, 0)),
                  pl.BlockSpec((heads, q_rank, 2 * LANES), lambda b, i: (0, 0, 0)),
                  pl.BlockSpec((heads, kv_rank, 2 * LANES), lambda b, i: (0, 0, 0))],
        out_specs=[pl.BlockSpec((1, heads, ts, MLA_QK), lambda b, i: (b, 0, i, 0)),
                   pl.BlockSpec((1, heads, MLA_QK, ts), lambda b, i: (b, 0, 0, i)),
                   pl.BlockSpec((1, heads, ts, MLA_V), lambda b, i: (b, 0, i, 0))],
        out_shape=[jax.ShapeDtypeStruct((batch, heads, seq, MLA_QK), BF16),
                   jax.ShapeDtypeStruct((batch, heads, MLA_QK, seq), BF16),
                   jax.ShapeDtypeStruct((batch, heads, seq, MLA_V), BF16)],
        compiler_params=_params(("parallel", "parallel")),
        name="mla_prep",
    )(p1, p1, p1, tab, gq.reshape(1, -1), gkv.reshape(1, -1), wq, wkv)


def _flash_body(q_ref, kt_ref, v_ref, o_ref, s_ref, m_ref, l_ref, acc_ref, *, tk, nk):
    m_ref[...] = jnp.full_like(m_ref, -jnp.inf)
    l_ref[...] = jnp.zeros_like(l_ref)
    acc_ref[...] = jnp.zeros_like(acc_ref)
    q = q_ref[0, 0]

    def key_rows(j):
        return slice(j * tk, (j + 1) * tk)

    def scores(j):
        return _dot(q, kt_ref[0, 0, :, key_rows(j)])

    s_ref[0] = scores(0)
    for j in range(nk):
        s = s_ref[j % 2]
        if j + 1 < nk:
            s_ref[(j + 1) % 2] = scores(j + 1)
        m_prev = m_ref[...]
        m_new = jnp.maximum(m_prev, jnp.max(s, axis=1, keepdims=True))
        alpha = jnp.exp2(m_prev - m_new)
        p = jnp.exp2(s - m_new[:, :1])
        l_ref[...] = alpha * l_ref[...] + jnp.sum(p, axis=1, keepdims=True)
        acc_ref[...] = alpha * acc_ref[...] + _dot(p.astype(BF16), v_ref[0, 0, key_rows(j), :])
        m_ref[...] = m_new
    o_ref[0] = (acc_ref[...] / l_ref[...]).astype(o_ref.dtype)


def _flash(q, k, v):
    B, H, S, _ = q.shape
    tq, tk = _tile(S, 1024), _tile(S, 1024)
    return pl.pallas_call(
        functools.partial(_flash_body, tk=tk, nk=S // tk),
        grid=(B, H, S // tq),
        in_specs=[pl.BlockSpec((1, 1, tq, MLA_QK), lambda b, h, i: (b, h, i, 0)),
                  pl.BlockSpec((1, 1, MLA_QK, S), lambda b, h, i: (b, h, 0, 0)),
                  pl.BlockSpec((1, 1, S, MLA_V), lambda b, h, i: (b, h, 0, 0))],
        out_specs=pl.BlockSpec((1, tq, MLA_V), lambda b, h, i: (b, i, h)),
        out_shape=jax.ShapeDtypeStruct((B, S, H * MLA_V), BF16),
        scratch_shapes=[pltpu.VMEM((2, tq, tk), F32), pltpu.VMEM((tq, LANES), F32),
                        pltpu.VMEM((tq, LANES), F32), pltpu.VMEM((tq, MLA_V), F32)],
        compiler_params=_params(("parallel", "parallel", "arbitrary")),
        name="mla_flash",
    )(q, k, v)


def _dn_conv_body(cur_ref, prev_ref, next_ref, w_ref, o_ref, buf_ref, *, ts, ns, qk_blocks):
    i = pl.program_id(1)
    c = pl.program_id(2)
    pad = DN_CONV // 2
    buf_ref[0:SUBLANES] = jnp.where(i == 0, 0.0, prev_ref[...])
    buf_ref[SUBLANES:SUBLANES + ts] = cur_ref[...]
    buf_ref[SUBLANES + ts:] = jnp.where(i == ns - 1, 0.0, next_ref[...])
    w = w_ref[...]
    acc = w[0:1] * buf_ref[SUBLANES - pad:SUBLANES - pad + ts]
    for t in range(1, DN_CONV):
        acc = acc + w[t:t + 1] * buf_ref[SUBLANES - pad + t:SUBLANES - pad + t + ts]
    y = acc * jax.nn.sigmoid(acc)
    kind = c // qk_blocks
    for g in range(y.shape[1] // DN_K):
        yg = y[:, g * DN_K:(g + 1) * DN_K]
        inv = lax.rsqrt(jnp.sum(yg * yg, axis=-1, keepdims=True) + EPS)
        inv = jnp.where(kind < 2, inv, 1.0) * jnp.where(kind == 0, DN_K ** -0.5, 1.0)
        o_ref[:, g * DN_K:(g + 1) * DN_K] = (yg * inv).astype(o_ref.dtype)


def _dn_conv(p2, conv_w, *, batch, seq, heads):
    width = 3 * heads * DN_K
    ts = _tile(seq, 512)
    ns = seq // ts
    cb = 4 * DN_K
    qk_blocks = heads * DN_K // cb
    r8 = ts // SUBLANES
    last8 = batch * seq // SUBLANES - 1
    return pl.pallas_call(
        functools.partial(_dn_conv_body, ts=ts, ns=ns, qk_blocks=qk_blocks),
        grid=(batch, ns, width // cb),
        in_specs=[pl.BlockSpec((ts, cb), lambda b, i, c: (b * ns + i, c)),
                  pl.BlockSpec((SUBLANES, cb), lambda b, i, c: (jnp.maximum((b * ns + i) * r8 - 1, 0), c)),
                  pl.BlockSpec((SUBLANES, cb), lambda b, i, c: (jnp.minimum((b * ns + i + 1) * r8, last8), c)),
                  pl.BlockSpec((DN_CONV, cb), lambda b, i, c: (0, c))],
        out_specs=pl.BlockSpec((ts, cb), lambda b, i, c: (b * ns + i, c)),
        out_shape=jax.ShapeDtypeStruct((batch * seq, width), BF16),
        scratch_shapes=[pltpu.VMEM((ts + 2 * SUBLANES, cb), F32)],
        compiler_params=_params(("parallel", "parallel", "parallel")),
        name="dn_conv",
    )(p2, p2, p2, conv_w)


def _dn_gate_body(x_ref, alog_ref, dtb_ref, o_ref, *, heads):
    x = x_ref[...]
    z = x + dtb_ref[...]
    softplus = jnp.maximum(z, 0.0) + jnp.log1p(jnp.exp(-jnp.abs(z)))
    g = -jnp.exp(alog_ref[...]) * softplus
    lane = lax.broadcasted_iota(jnp.int32, x.shape, 1)
    zt = jnp.where(lane < LANES - 2 * heads, g, jax.nn.sigmoid(x)).T
    o_ref[0, 2 * heads:] = zt[LANES - 2 * heads:]
    pos = lax.broadcasted_iota(jnp.int32, (2 * heads, LANES), 1) % CHUNK
    fwd = lax.broadcasted_iota(jnp.int32, (2 * heads, LANES), 0) < heads
    for grp in range(zt.shape[1] // LANES):
        cols = slice(grp * LANES, (grp + 1) * LANES)
        gt = zt[LANES - 4 * heads:LANES - 2 * heads, cols]
        pre, suf = gt, gt
        s = 1
        while s < CHUNK:
            pre = pre + jnp.where(pos >= s, pltpu.roll(pre, s, 1), 0.0)
            suf = suf + jnp.where(pos < CHUNK - s, pltpu.roll(suf, LANES - s, 1), 0.0)
            s *= 2
        o_ref[0, :2 * heads, cols] = jnp.where(fwd, pre, suf)


def _dn_gates(p1, a_log, dt_bias, *, batch, seq, heads, col_block):
    ts = _tile(seq, 512)
    ns = seq // ts
    lo, hi = LANES - 4 * heads, LANES - 2 * heads
    alog = jnp.zeros((1, LANES), F32).at[0, lo:hi].set(a_log.reshape(-1))
    dtb = jnp.zeros((1, LANES), F32).at[0, lo:hi].set(dt_bias.reshape(-1))
    return pl.pallas_call(
        functools.partial(_dn_gate_body, heads=heads),
        grid=(batch, ns),
        in_specs=[pl.BlockSpec((ts, LANES), lambda b, i: (b * ns + i, col_block)),
                  pl.BlockSpec((1, LANES), lambda b, i: (0, 0)),
                  pl.BlockSpec((1, LANES), lambda b, i: (0, 0))],
        out_specs=pl.BlockSpec((1, 4 * heads, ts), lambda b, i: (b, 0, i)),
        out_shape=jax.ShapeDtypeStruct((batch, 4 * heads, seq), F32),
        compiler_params=_params(("parallel", "parallel")),
        name="dn_gates",
    )(p1, alog, dtb)


def _split_dot(a, b):
    ah, bh = a.astype(BF16), b.astype(BF16)
    al = (a - ah.astype(F32)).astype(BF16)
    bl = (b - bh.astype(F32)).astype(BF16)
    return _dot(ah, bh) + _dot(ah, bl) + _dot(al, bh)


def _unit_triangular_inverses(ms, eye):
    a = [eye + m for m in ms]
    ab = [x.astype(BF16) for x in a]
    t = [eye - m for m in ms]
    for _ in range(int(math.log2(CHUNK)) - 2):
        r = [2.0 * eye - _dot(x, y.astype(BF16)) for x, y in zip(ab, t)]
        t = [_dot(x.astype(BF16), y.astype(BF16)) for x, y in zip(t, r)]
    r = [eye - _split_dot(x, y) for x, y in zip(a, t)]
    return [x + _dot(x.astype(BF16), y.astype(BF16)) for x, y in zip(t, r)]


def _dn_local_body(q_ref, k_ref, v_ref, grow_ref, glast_ref,
                   u_ref, w_ref, qd_ref, kd_ref, qk_ref, *, chunks):
    ii = lax.broadcasted_iota(jnp.int32, (CHUNK, CHUNK), 0)
    jj = lax.broadcasted_iota(jnp.int32, (CHUNK, CHUNK), 1)
    eye = (ii == jj).astype(F32)
    rows = [slice(c * CHUNK, (c + 1) * CHUNK) for c in range(chunks)]
    q = [q_ref[r, :] for r in rows]
    k = [k_ref[r, :] for r in rows]
    v = [v_ref[r, :] for r in rows]
    qk_kk = [_dot_nt(jnp.concatenate([x, y], axis=0), y) for x, y in zip(q, k)]

    chains = [(c, d) for c in range(chunks) for d in range(2)]
    g_row, b_row, g_col, g_last, decay, ms = [], [], [], [], [], []
    for c, d in chains:
        incl = (jj <= ii) if d == 0 else (jj >= ii)
        strict = (jj < ii) if d == 0 else (jj > ii)
        r = jnp.broadcast_to(grow_ref[0, d, 0, c:c + 1, :], (LANES, LANES))
        rt = r.T
        gc, bc = rt[:CHUNK], rt[CHUNK:]
        gr = r[:CHUNK, :CHUNK]
        dec = jnp.exp(jnp.where(incl, gc[:, :CHUNK] - gr, -jnp.inf))
        ms.append(jnp.where(strict, qk_kk[c][CHUNK:] * bc[:, :CHUNK] * dec, 0.0))
        g_row.append(gr)
        b_row.append(r[:CHUNK, CHUNK:])
        g_col.append(gc)
        g_last.append(jnp.broadcast_to(glast_ref[0, d, 0, c:c + 1, :], (CHUNK, LANES)))
        decay.append(dec)

    t_beta = [t * b for t, b in zip(_unit_triangular_inverses(ms, eye), b_row)]
    us = [_dot(t.astype(BF16), v[c]) for t, (c, d) in zip(t_beta, chains)]
    ws = [_dot((t * jnp.exp(g)).astype(BF16), k[c]) for t, g, (c, d) in zip(t_beta, g_row, chains)]
    for i, (c, d) in enumerate(chains):
        u_ref[0, d, 0, rows[c], :] = us[i]
        w_ref[0, d, 0, rows[c], :] = ws[i].astype(BF16)
        qk_ref[0, d, 0, rows[c], :] = (qk_kk[c][:CHUNK] * decay[i]).astype(BF16)
        qd_ref[0, d, 0, rows[c], :] = (q[c].astype(F32) * jnp.exp(g_col[i])).astype(BF16)
        kd_ref[0, d, 0, rows[c], :] = (k[c].astype(F32) * jnp.exp(g_last[i] - g_col[i])).astype(BF16)


def _dn_local(qkv, grow, glast, *, batch, seq, heads):
    nc = seq // CHUNK
    cbk = _tile(nc, 16)
    L = cbk * CHUNK
    nb = nc // cbk
    tok = lambda off: pl.BlockSpec((L, DN_K), lambda b, h, t: (b * nb + t, off + h))
    gate = pl.BlockSpec((1, 2, 1, cbk, LANES), lambda b, h, t: (b, 0, h, t, 0))
    out = lambda width: pl.BlockSpec((1, 2, 1, L, width), lambda b, h, t: (b, 0, h, t, 0))
    shp = lambda width, dt: jax.ShapeDtypeStruct((batch, 2, heads, seq, width), dt)
    return pl.pallas_call(
        functools.partial(_dn_local_body, chunks=cbk),
        grid=(batch, heads, nb),
        in_specs=[tok(0), tok(heads), tok(2 * heads), gate, gate],
        out_specs=[out(DN_V), out(DN_K), out(DN_K), out(DN_K), out(CHUNK)],
        out_shape=[shp(DN_V, F32), shp(DN_K, BF16), shp(DN_K, BF16), shp(DN_K, BF16), shp(CHUNK, BF16)],
        compiler_params=_params(("parallel", "parallel", "parallel")),
        name="dn_local",
    )(qkv, qkv, qkv, grow, glast)


def _dn_scan_body(*refs, chunks, hb):
    ins, (of_ref, ob_ref, st_ref) = refs[:12], refs[12:]
    t = pl.program_id(2)

    @pl.when(t == 0)
    def _():
        st_ref[...] = jnp.zeros_like(st_ref)

    chains = [(d, j) for d in range(2) for j in range(hb)]
    states = [st_ref[d, j] for d, j in chains]
    for ci in range(chunks):
        blk = []
        for d, j in chains:
            u_ref, w_ref, qd_ref, kd_ref, qk_ref, gl_ref = ins[6 * d:6 * d + 6]
            c = ci if d == 0 else chunks - 1 - ci
            rows = slice(c * CHUNK, (c + 1) * CHUNK)
            blk.append((u_ref[0, 0, j, rows, :], w_ref[0, 0, j, rows, :], qd_ref[0, 0, j, rows, :],
                        kd_ref[0, 0, j, rows, :], qk_ref[0, 0, j, rows, :],
                        jnp.exp(gl_ref[0, 0, j, c:c + 1, :]), rows))
        sb = [s.astype(BF16) for s in states]
        w_s = [_dot(b[1], s) for b, s in zip(blk, sb)]
        q_s = [_dot(b[2], s) for b, s in zip(blk, sb)]
        vb = [(b[0] - x).astype(BF16) for b, x in zip(blk, w_s)]
        outs = [x + _dot(b[4], y) for b, x, y in zip(blk, q_s, vb)]
        states = [s * b[5] + _dot_tn(b[3], y) for b, s, y in zip(blk, states, vb)]
        for (d, j), b, o in zip(chains, blk, outs):
            o_ref = of_ref if d == 0 else ob_ref
            o_ref[b[6], j * DN_V:(j + 1) * DN_V] = o
    for (d, j), s in zip(chains, states):
        st_ref[d, j] = s


def _dn_scan(u, w, qd, kd, qk, glast, *, batch, seq, heads):
    nc = seq // CHUNK
    cbk = _tile(nc, 8)
    L = cbk * CHUNK
    nb = nc // cbk
    hb = _tile(heads, 8)

    def spec(d, rows, width):
        if d == 0:
            return pl.BlockSpec((1, 1, hb, rows, width), lambda b, g, t: (b, 0, g, t, 0))
        return pl.BlockSpec((1, 1, hb, rows, width), lambda b, g, t: (b, 1, g, nb - 1 - t, 0))

    in_specs, args = [], []
    for d in range(2):
        in_specs += [spec(d, L, DN_V), spec(d, L, DN_K), spec(d, L, DN_K), spec(d, L, DN_K),
                     spec(d, L, CHUNK), spec(d, cbk, LANES)]
        args += [u, w, qd, kd, qk, glast]
    out_shape = jax.ShapeDtypeStruct((batch * seq, heads * DN_V), F32)
    return pl.pallas_call(
        functools.partial(_dn_scan_body, chunks=cbk, hb=hb),
        grid=(batch, heads // hb, nb),
        in_specs=in_specs,
        out_specs=[pl.BlockSpec((L, hb * DN_V), lambda b, g, t: (b * nb + t, g)),
                   pl.BlockSpec((L, hb * DN_V), lambda b, g, t: (b * nb + nb - 1 - t, g))],
        out_shape=[out_shape, out_shape],
        scratch_shapes=[pltpu.VMEM((2, hb, DN_K, DN_V), F32)],
        compiler_params=_params(("parallel", "parallel", "arbitrary")),
        name="dn_scan",
    )(*args)


def _dn_out_body(of_ref, ob_ref, z_ref, g_ref, o_ref):
    o = of_ref[...] + ob_ref[...]
    z = z_ref[...]
    g = g_ref[...]
    for h in range(o.shape[1] // DN_V):
        sl = slice(h * DN_V, (h + 1) * DN_V)
        oh, zh = o[:, sl], z[:, sl]
        y = oh * lax.rsqrt(jnp.mean(oh * oh, axis=-1, keepdims=True) + EPS) * g
        o_ref[:, sl] = (y * (zh * jax.nn.sigmoid(zh))).astype(o_ref.dtype)


def _dn_out(o_f, o_b, p2, norm_g, *, z_col0):
    M, W = o_f.shape
    tm, tw = _tile(M, 512), _tile(W, 512)
    zj = z_col0 // tw
    return pl.pallas_call(
        _dn_out_body,
        grid=(M // tm, W // tw),
        in_specs=[pl.BlockSpec((tm, tw), lambda i, j: (i, j)),
                  pl.BlockSpec((tm, tw), lambda i, j: (i, j)),
                  pl.BlockSpec((tm, tw), lambda i, j: (i, zj + j)),
                  pl.BlockSpec((1, DN_V), lambda i, j: (0, 0))],
        out_specs=pl.BlockSpec((tm, tw), lambda i, j: (i, j)),
        out_shape=jax.ShapeDtypeStruct((M, W), BF16),
        compiler_params=_params(("parallel", "parallel")),
        name="dn_out",
    )(o_f, o_b, p2, norm_g.reshape(1, DN_V))


def _rope_table(positions):
    half = ROPE // 2
    inv_freq = ROPE_THETA ** (-jnp.arange(half, dtype=F32) / half)
    ang = positions.astype(F32)[..., None] * inv_freq
    cos, sin = jnp.cos(ang), jnp.sin(ang)
    return jnp.concatenate([cos, cos, sin, sin], axis=-1).reshape(-1, 4 * half)


def _mixer(u, tab, w_in, q_norm_g, kv_norm_g, w_uq, w_ukv, w_branch_a, conv_w, a_log, dt_bias,
           o_norm_g, w_branch_b, w_out, *, batch, seq, x, gate, alpha):
    d_model = u.shape[1]
    q_rank, kv_rank = q_norm_g.shape[0], kv_norm_g.shape[0]
    mla_heads = w_uq.shape[1] // MLA_QK
    dn_heads = a_log.shape[1]
    dn_qk_w, dn_v_w = dn_heads * DN_K, dn_heads * DN_V
    c_qkv = q_rank + kv_rank + ROPE
    c_z = c_qkv + 2 * dn_qk_w + dn_v_w
    c_a = c_z + dn_v_w
    c_gate = c_a + 4 * dn_heads
    assert ROPE + 4 * dn_heads == LANES and c_qkv - ROPE == (q_rank + kv_rank)
    w1 = jnp.concatenate([w_in[:, :c_qkv], w_in[:, c_a:c_gate]], axis=1).astype(BF16)
    w2 = jnp.concatenate([w_in[:, c_qkv:c_a], w_in[:, c_gate:]], axis=1).astype(BF16)
    p1 = _matmul(u, w1, F32, tm=512, tn=w1.shape[1])
    p2 = _matmul(u, w2, F32)

    wq = w_uq.reshape(q_rank, mla_heads, MLA_QK)
    r1, r2 = wq[..., NOPE:NOPE + ROPE // 2], wq[..., NOPE + ROPE // 2:]
    wq = jnp.concatenate([wq, -r2, r1], axis=-1).transpose(1, 0, 2).astype(BF16)
    wkv = w_ukv.reshape(kv_rank, mla_heads, NOPE + MLA_V).transpose(1, 0, 2).astype(BF16)
    q, k, v = _mla_prep(p1, tab, q_norm_g, kv_norm_g, wq, wkv, batch=batch, seq=seq)
    o_a = _flash(q, k, v).reshape(batch * seq, mla_heads * MLA_V)

    qkv = _dn_conv(p2, conv_w, batch=batch, seq=seq, heads=dn_heads)
    gates = _dn_gates(p1, a_log, dt_bias, batch=batch, seq=seq, heads=dn_heads,
                      col_block=(q_rank + kv_rank) // LANES)
    nc = seq // CHUNK
    gates = gates.reshape(batch, 2, 2, dn_heads, nc, CHUNK)
    grow = gates.transpose(0, 2, 3, 4, 1, 5).reshape(batch, 2, dn_heads, nc, 2 * CHUNK)
    g_cum = gates[:, 0]
    glast = jnp.stack([g_cum[:, 0, :, :, CHUNK - 1], g_cum[:, 1, :, :, 0]], axis=1)
    glast = jnp.broadcast_to(glast[..., None], glast.shape + (LANES,))
    u_, w_, qd, kd, qk = _dn_local(qkv, grow, glast, batch=batch, seq=seq, heads=dn_heads)
    o_f, o_b = _dn_scan(u_, w_, qd, kd, qk, glast, batch=batch, seq=seq, heads=dn_heads)
    o_bn = _dn_out(o_f, o_b, p2, o_norm_g, z_col0=2 * dn_qk_w + dn_v_w)

    merged = _merge(o_a, o_bn, w_branch_a.astype(BF16), w_branch_b.astype(BF16), p2, d_model,
                    gate_col0=2 * dn_qk_w + 2 * dn_v_w)
    return _matmul(merged, w_out.astype(BF16), F32, tn=512, x=x, gate=gate, seq=seq, residual=(alpha, 1.0))


def _ffn(h, w_gate, w_up, w_down, layer, *, x, gate, seq, alpha):
    a = _gateup(h, w_gate, w_up, layer)
    return _matmul(a, w_down[layer].astype(BF16), F32, tm=512, tn=512, tk=w_down.shape[1],
                   x=x, gate=gate, seq=seq, residual=(alpha, 0.5))


def kernel(x, c, positions, ln_in_g, ln_in_b, w_ada, b_ada, ada_table, ffn1_w_gate, ffn1_w_up, ffn1_w_down, w_in, mla_q_norm_g, mla_kv_norm_g, mla_w_uq, mla_w_ukv, w_branch_a, dn_conv_w, dn_a_log, dn_dt_bias, dn_norm_g, w_branch_b, w_out, ffn2_w_gate, ffn2_w_up, ffn2_w_down, post_ln_g, post_ln_b):
    B, S, D = x.shape
    depth = ada_table.shape[0]
    alpha = (2.0 * depth) ** 0.25
    cond = _adaln(c, w_ada, b_ada).reshape(B, N_SUB, 3, D)
    mod = cond[None] + ada_table[:, None]
    tab = _rope_table(positions)
    post = functools.partial(_post, seq=S)

    xf, h = post(x.reshape(B * S, D), ln_in_g, ln_in_b, mod[0, :, 0, 0], mod[0, :, 0, 1])
    for l in range(depth):
        xs = _ffn(h, ffn1_w_gate, ffn1_w_up, ffn1_w_down, l, x=xf, gate=mod[l, :, 0, 2], seq=S, alpha=alpha)
        xf, h = post(xs, post_ln_g[l, 0], post_ln_b[l, 0], mod[l, :, 1, 0], mod[l, :, 1, 1])
        xs = _mixer(h, tab, w_in[l], mla_q_norm_g[l], mla_kv_norm_g[l], mla_w_uq[l], mla_w_ukv[l],
                    w_branch_a[l], dn_conv_w[l], dn_a_log[l], dn_dt_bias[l], dn_norm_g[l],
                    w_branch_b[l], w_out[l], batch=B, seq=S, x=xf, gate=mod[l, :, 1, 2], alpha=alpha)
        xf, h = post(xs, post_ln_g[l, 1], post_ln_b[l, 1], mod[l, :, 2, 0], mod[l, :, 2, 1])
        xs = _ffn(h, ffn2_w_gate, ffn2_w_up, ffn2_w_down, l, x=xf, gate=mod[l, :, 2, 2], seq=S, alpha=alpha)
        last = l == depth - 1
        xf, h = post(xs, post_ln_g[l, 2], post_ln_b[l, 2],
                     None if last else mod[l + 1, :, 0, 0], None if last else mod[l + 1, :, 0, 1])
    return xf.reshape(B, S, D)
```

```python
import functools
import math

import jax
import jax.numpy as jnp
from jax import lax
from jax.experimental import pallas as pl
from jax.experimental.pallas import tpu as pltpu

F32 = jnp.float32
BF16 = jnp.bfloat16

NOPE = 128
ROPE = 64
MLA_V = 128
MLA_QK = NOPE + ROPE
DN_K = 128
DN_V = 128
DN_CONV = 5
CHUNK = 64
N_SUB = 3
EPS = 1e-6
ROPE_THETA = 10000.0
LOG2E = math.log2(math.e)

LANES = 128
SUBLANES = 8
V7X_VMEM_LIMIT_BYTES = 56 * 1024 * 1024


def _tile(dim, pref):
    t = min(dim, pref)
    assert dim % t == 0, (dim, pref)
    return t


def _params(sem):
    return pltpu.CompilerParams(dimension_semantics=sem, vmem_limit_bytes=V7X_VMEM_LIMIT_BYTES)


def _dot(a, b):
    return jnp.dot(a, b, preferred_element_type=F32)


def _dot_nt(a, b):
    return lax.dot_general(a, b, (((1,), (1,)), ((), ())), preferred_element_type=F32)


def _dot_tn(a, b):
    return lax.dot_general(a, b, (((0,), (0,)), ((), ())), preferred_element_type=F32)


def _adaln_body(c_ref, w_ref, b_ref, o_ref):
    c = c_ref[...]
    h = (c * jax.nn.sigmoid(c)).astype(BF16)
    o_ref[...] = _dot(h, w_ref[...].astype(BF16)) + b_ref[...]


def _adaln(c, w_ada, b_ada):
    B, D = c.shape
    N = w_ada.shape[1]
    rows = max(SUBLANES, B)
    c_pad = jnp.zeros((rows, D), F32).at[:B].set(c)
    tn = _tile(N, 512)
    out = pl.pallas_call(
        _adaln_body,
        grid=(N // tn,),
        in_specs=[pl.BlockSpec((rows, D), lambda j: (0, 0)),
                  pl.BlockSpec((D, tn), lambda j: (0, j)),
                  pl.BlockSpec((1, tn), lambda j: (0, j))],
        out_specs=pl.BlockSpec((rows, tn), lambda j: (0, j)),
        out_shape=jax.ShapeDtypeStruct((rows, N), F32),
        compiler_params=_params(("parallel",)),
        name="adaln",
    )(c_pad, w_ada, b_ada.reshape(1, N))
    return out[:B]


def _post_body(*refs, has_mod):
    it = iter(refs)
    x_ref, g_ref, b_ref = next(it), next(it), next(it)
    if has_mod:
        sh_ref, sc_ref, h_ref, st_ref = next(it), next(it), next(it), next(it)
    else:
        xo_ref = next(it)
    x = x_ref[...]
    mu = jnp.mean(x, axis=-1, keepdims=True)
    xc = x - mu
    var = jnp.mean(xc * xc, axis=-1, keepdims=True)
    rstd = lax.rsqrt(var + EPS)
    y = xc * rstd * g_ref[...] + b_ref[...]
    if has_mod:
        h_ref[...] = (y * (1.0 + sc_ref[0]) + sh_ref[0]).astype(BF16)
        st_ref[:, :LANES] = jnp.broadcast_to(mu, (mu.shape[0], LANES))
        st_ref[:, LANES:] = jnp.broadcast_to(rstd, (rstd.shape[0], LANES))
    else:
        xo_ref[...] = y


def _post(x, g, b, shift, scale, *, seq):
    M, D = x.shape
    has_mod = shift is not None
    tm = _tile(seq, 256)
    per_b = seq // tm
    row = pl.BlockSpec((tm, D), lambda i: (i, 0))
    vec = pl.BlockSpec((1, D), lambda i: (0, 0))
    bvec = pl.BlockSpec((1, 1, D), lambda i: (i // per_b, 0, 0))
    args = [x, g.reshape(1, D), b.reshape(1, D)]
    specs = [row, vec, vec]
    if has_mod:
        args += [shift.reshape(-1, 1, D), scale.reshape(-1, 1, D)]
        specs += [bvec, bvec]
        out_shape = [jax.ShapeDtypeStruct((M, D), BF16), jax.ShapeDtypeStruct((M, 2 * LANES), F32)]
        out_specs = [row, pl.BlockSpec((tm, 2 * LANES), lambda i: (i, 0))]
    else:
        out_shape = [jax.ShapeDtypeStruct((M, D), F32)]
        out_specs = [row]
    outs = pl.pallas_call(
        functools.partial(_post_body, has_mod=has_mod),
        grid=(M // tm,),
        in_specs=specs,
        out_specs=out_specs,
        out_shape=out_shape,
        compiler_params=_params(("parallel",)),
        name="post_ln",
    )(*args)
    return outs if has_mod else outs[0]


def _mm_body(*refs, nk, residual):
    a_ref, w_ref = refs[:2]
    refs = refs[2:]
    if residual is not None:
        (x_ref, st_ref, lg_ref, lb_ref, gate_ref), refs = refs[:5], refs[5:]
    o_ref = refs[0]

    def finish(acc):
        if residual is None:
            o_ref[...] = acc.astype(o_ref.dtype)
            return
        alpha, r = residual
        mu, rstd = st_ref[:, :LANES], st_ref[:, LANES:]
        for s in range(acc.shape[1] // LANES):
            sl = slice(s * LANES, (s + 1) * LANES)
            xn = (x_ref[:, sl] - mu) * rstd * lg_ref[:, sl] + lb_ref[:, sl]
            o_ref[:, sl] = alpha * xn + (r * gate_ref[0][:, sl]) * acc[:, sl]

    if nk == 1:
        finish(_dot(a_ref[...], w_ref[...]))
        return
    acc_ref = refs[1]
    k = pl.program_id(2)

    @pl.when(k == 0)
    def _():
        acc_ref[...] = jnp.zeros_like(acc_ref)

    acc_ref[...] += _dot(a_ref[...], w_ref[...])

    @pl.when(k == nk - 1)
    def _():
        finish(acc_ref[...])


def _matmul(a, w, out_dtype, *, tm=1024, tn=1024, tk=4096, stream=None, gate=None, seq=None, residual=None):
    M, K = a.shape
    N = w.shape[1]
    tm, tn, tk = _tile(M if residual is None else seq, tm), _tile(N, tn), _tile(K, tk)
    nk = K // tk
    args = [a, w]
    in_specs = [pl.BlockSpec((tm, tk), lambda i, j, k: (i, k)),
                pl.BlockSpec((tk, tn), lambda i, j, k: (k, j))]
    if residual is not None:
        per_b = seq // tm
        x, stats, g, b = stream
        args += [x, stats, g.reshape(1, N), b.reshape(1, N), gate.reshape(-1, 1, N)]
        in_specs += [pl.BlockSpec((tm, tn), lambda i, j, k: (i, j)),
                     pl.BlockSpec((tm, 2 * LANES), lambda i, j, k: (i, 0)),
                     pl.BlockSpec((1, tn), lambda i, j, k: (0, j)),
                     pl.BlockSpec((1, tn), lambda i, j, k: (0, j)),
                     pl.BlockSpec((1, 1, tn), lambda i, j, k: (i // per_b, 0, j))]
    return pl.pallas_call(
        functools.partial(_mm_body, nk=nk, residual=residual),
        grid=(M // tm, N // tn, nk),
        in_specs=in_specs,
        out_specs=pl.BlockSpec((tm, tn), lambda i, j, k: (i, j)),
        out_shape=jax.ShapeDtypeStruct((M, N), out_dtype),
        scratch_shapes=[pltpu.VMEM((tm, tn), F32)] if nk > 1 else [],
        compiler_params=_params(("parallel", "parallel", "arbitrary")),
        name="matmul",
    )(*args)


def _gateup_body(a_ref, wg_ref, wu_ref, o_ref, wgb_ref, wub_ref):
    @pl.when(pl.program_id(1) == 0)
    def _():
        wgb_ref[...] = wg_ref[0].astype(BF16)
        wub_ref[...] = wu_ref[0].astype(BF16)

    a = a_ref[...]
    g = _dot(a, wgb_ref[...])
    u = _dot(a, wub_ref[...])
    o_ref[...] = (g * jax.nn.sigmoid(g) * u).astype(o_ref.dtype)


def _gateup(a, wg, wu, layer):
    M, K = a.shape
    N = wg.shape[2]
    tm, tn = _tile(M, 1024), _tile(N, 512)
    weight = pl.BlockSpec((1, K, tn), lambda j, i: (layer, 0, j), pipeline_mode=pl.Buffered(1))
    return pl.pallas_call(
        _gateup_body,
        grid=(N // tn, M // tm),
        in_specs=[pl.BlockSpec((tm, K), lambda j, i: (i, 0)), weight, weight],
        out_specs=pl.BlockSpec((tm, tn), lambda j, i: (i, j)),
        out_shape=jax.ShapeDtypeStruct((M, N), BF16),
        scratch_shapes=[pltpu.VMEM((K, tn), BF16), pltpu.VMEM((K, tn), BF16)],
        compiler_params=_params(("parallel", "arbitrary")),
        name="ffn_gate_up",
    )(a, wg, wu)


def _merge_body(oa_ref, ob_ref, wa_ref, wb_ref, ga_ref, gb_ref, o_ref):
    ya = _dot(oa_ref[...], wa_ref[...])
    yb = _dot(ob_ref[...], wb_ref[...])
    o_ref[...] = (jax.nn.sigmoid(ga_ref[...]) * ya + jax.nn.sigmoid(gb_ref[...]) * yb).astype(o_ref.dtype)


def _merge(oa, ob, wa, wb, p2, d_model, gate_col0):
    M, Ka = oa.shape
    Kb = ob.shape[1]
    tm, tn = _tile(M, 1024), _tile(d_model, 512)
    ja = gate_col0 // tn
    jb = (gate_col0 + d_model) // tn
    return pl.pallas_call(
        _merge_body,
        grid=(M // tm, d_model // tn),
        in_specs=[pl.BlockSpec((tm, Ka), lambda i, j: (i, 0)),
                  pl.BlockSpec((tm, Kb), lambda i, j: (i, 0)),
                  pl.BlockSpec((Ka, tn), lambda i, j: (0, j)),
                  pl.BlockSpec((Kb, tn), lambda i, j: (0, j)),
                  pl.BlockSpec((tm, tn), lambda i, j: (i, ja + j)),
                  pl.BlockSpec((tm, tn), lambda i, j: (i, jb + j))],
        out_specs=pl.BlockSpec((tm, tn), lambda i, j: (i, j)),
        out_shape=jax.ShapeDtypeStruct((M, d_model), BF16),
        compiler_params=_params(("parallel", "parallel")),
        name="branch_merge",
    )(oa, ob, wa, wb, p2, p2)


def _rms(x, g):
    return x * lax.rsqrt(jnp.mean(x * x, axis=-1, keepdims=True) + EPS) * g


def _mla_prep_body(ql_ref, kvl_ref, kr_ref, tab_ref, gq_ref, gkv_ref, wq_ref, wkv_ref,
                   q_ref, k_ref, v_ref, *, heads, q_scale):
    qn = _rms(ql_ref[...], gq_ref[...]).astype(BF16)
    kvn = _rms(kvl_ref[...], gkv_ref[...]).astype(BF16)
    tab = tab_ref[...]
    tab_sin = pltpu.roll(tab, 64, 1)
    x = kr_ref[...]
    lane = lax.broadcasted_iota(jnp.int32, x.shape, 1)
    x_rot = jnp.where(lane < ROPE // 2, -pltpu.roll(x, LANES - ROPE // 2, 1), pltpu.roll(x, ROPE // 2, 1))
    k_rope_t = (x * tab + x_rot * tab_sin).T[:ROPE].astype(BF16)
    for h in range(heads):
        r = _dot(qn, wq_ref[h])
        p = r[:, NOPE:] * tab
        roped = p + pltpu.roll(p, 64, 1)
        q_ref[0, h, :, :NOPE] = (r[:, :NOPE] * q_scale).astype(BF16)
        q_ref[0, h, :, NOPE:] = (roped[:, :ROPE] * q_scale).astype(BF16)
        kv = _dot(kvn, wkv_ref[h])
        k_ref[0, h, :NOPE, :] = kv[:, :NOPE].T.astype(BF16)
        k_ref[0, h, NOPE:, :] = k_rope_t
        v_ref[0, h] = kv[:, NOPE:].astype(BF16)


def _mla_prep(p1, tab, gq, gkv, wq, wkv, *, batch, seq):
    heads, q_rank, _ = wq.shape
    kv_rank = wkv.shape[1]
    ts = _tile(seq, 512)
    ns = seq // ts
    row = lambda b, i: b * ns + i
    return pl.pallas_call(
        functools.partial(_mla_prep_body, heads=heads, q_scale=MLA_QK ** -0.5 * LOG2E),
        grid=(batch, ns),
        in_specs=[pl.BlockSpec((ts, q_rank), lambda b, i: (row(b, i), 0)),
                  pl.BlockSpec((ts, kv_rank), lambda b, i: (row(b, i), q_rank // kv_rank)),
                  pl.BlockSpec((ts, LANES), lambda b, i: (row(b, i), (q_rank + kv_rank) // LANES)),
                  pl.BlockSpec((ts, LANES), lambda b, i: (row(b, i), 0)),
                  pl.BlockSpec((1, q_rank), lambda b, i: (0, 0)),
                  pl.BlockSpec((1, kv_rank), lambda b, i: (0, 0)),
                  pl.BlockSpec((heads, q_rank, 2 * LANES), lambda b, i: (0, 0, 0)),
                  pl.BlockSpec((heads, kv_rank, 2 * LANES), lambda b, i: (0, 0, 0))],
        out_specs=[pl.BlockSpec((1, heads, ts, MLA_QK), lambda b, i: (b, 0, i, 0)),
                   pl.BlockSpec((1, heads, MLA_QK, ts), lambda b, i: (b, 0, 0, i)),
                   pl.BlockSpec((1, heads, ts, MLA_V), lambda b, i: (b, 0, i, 0))],
        out_shape=[jax.ShapeDtypeStruct((batch, heads, seq, MLA_QK), BF16),
                   jax.ShapeDtypeStruct((batch, heads, MLA_QK, seq), BF16),
                   jax.ShapeDtypeStruct((batch, heads, seq, MLA_V), BF16)],
        compiler_params=_params(("parallel", "parallel")),
        name="mla_prep",
    )(p1, p1, p1, tab, gq.reshape(1, -1), gkv.reshape(1, -1), wq, wkv)


def _flash_body(q_ref, kt_ref, v_ref, o_ref, s_ref, m_ref, l_ref, acc_ref, *, tk, nk):
    m_ref[...] = jnp.full_like(m_ref, -jnp.inf)
    l_ref[...] = jnp.zeros_like(l_ref)
    acc_ref[...] = jnp.zeros_like(acc_ref)
    q = q_ref[0, 0]

    def key_rows(j):
        return slice(j * tk, (j + 1) * tk)

    def scores(j):
        return _dot(q, kt_ref[0, 0, :, key_rows(j)])

    s_ref[0] = scores(0)
    for j in range(nk):
        s = s_ref[j % 2]
        if j + 1 < nk:
            s_ref[(j + 1) % 2] = scores(j + 1)
        m_prev = m_ref[...]
        m_new = jnp.maximum(m_prev, jnp.max(s, axis=1, keepdims=True))
        alpha = jnp.exp2(m_prev - m_new)
        p = jnp.exp2(s - m_new[:, :1])
        l_ref[...] = alpha * l_ref[...] + jnp.sum(p, axis=1, keepdims=True)
        acc_ref[...] = alpha * acc_ref[...] + _dot(p.astype(BF16), v_ref[0, 0, key_rows(j), :])
        m_ref[...] = m_new
    o_ref[0] = (acc_ref[...] / l_ref[...]).astype(o_ref.dtype)


def _flash(q, k, v):
    B, H, S, _ = q.shape
    tq, tk = _tile(S, 1024), _tile(S, 1024)
    return pl.pallas_call(
        functools.partial(_flash_body, tk=tk, nk=S // tk),
        grid=(B, H, S // tq),
        in_specs=[pl.BlockSpec((1, 1, tq, MLA_QK), lambda b, h, i: (b, h, i, 0)),
                  pl.BlockSpec((1, 1, MLA_QK, S), lambda b, h, i: (b, h, 0, 0)),
                  pl.BlockSpec((1, 1, S, MLA_V), lambda b, h, i: (b, h, 0, 0))],
        out_specs=pl.BlockSpec((1, tq, MLA_V), lambda b, h, i: (b, i, h)),
        out_shape=jax.ShapeDtypeStruct((B, S, H * MLA_V), BF16),
        scratch_shapes=[pltpu.VMEM((2, tq, tk), F32), pltpu.VMEM((tq, LANES), F32),
                        pltpu.VMEM((tq, LANES), F32), pltpu.VMEM((tq, MLA_V), F32)],
        compiler_params=_params(("parallel", "parallel", "arbitrary")),
        name="mla_flash",
    )(q, k, v)


def _dn_conv_body(cur_ref, prev_ref, next_ref, w_ref, o_ref, buf_ref, *, ts, ns, qk_blocks):
    i = pl.program_id(1)
    c = pl.program_id(2)
    pad = DN_CONV // 2
    buf_ref[0:SUBLANES] = jnp.where(i == 0, 0.0, prev_ref[...])
    buf_ref[SUBLANES:SUBLANES + ts] = cur_ref[...]
    buf_ref[SUBLANES + ts:] = jnp.where(i == ns - 1, 0.0, next_ref[...])
    w = w_ref[...]
    full = buf_ref[...]
    rows = full.shape[0]
    acc = w[pad:pad + 1] * full[SUBLANES:SUBLANES + ts]
    for t in range(DN_CONV):
        if t != pad:
            acc = acc + w[t:t + 1] * pltpu.roll(full, (pad - t) % rows, 0)[SUBLANES:SUBLANES + ts]
    y = acc * jax.nn.sigmoid(acc)
    kind = c // qk_blocks
    for g in range(y.shape[1] // DN_K):
        yg = y[:, g * DN_K:(g + 1) * DN_K]
        inv = lax.rsqrt(jnp.sum(yg * yg, axis=-1, keepdims=True) + EPS)
        inv = jnp.where(kind < 2, inv, 1.0) * jnp.where(kind == 0, DN_K ** -0.5, 1.0)
        o_ref[:, g * DN_K:(g + 1) * DN_K] = (yg * inv).astype(o_ref.dtype)


def _dn_conv(p2, conv_w, *, batch, seq, heads):
    width = 3 * heads * DN_K
    ts = _tile(seq, 512)
    ns = seq // ts
    cb = 4 * DN_K
    qk_blocks = heads * DN_K // cb
    r8 = ts // SUBLANES
    last8 = batch * seq // SUBLANES - 1
    return pl.pallas_call(
        functools.partial(_dn_conv_body, ts=ts, ns=ns, qk_blocks=qk_blocks),
        grid=(batch, ns, width // cb),
        in_specs=[pl.BlockSpec((ts, cb), lambda b, i, c: (b * ns + i, c)),
                  pl.BlockSpec((SUBLANES, cb), lambda b, i, c: (jnp.maximum((b * ns + i) * r8 - 1, 0), c)),
                  pl.BlockSpec((SUBLANES, cb), lambda b, i, c: (jnp.minimum((b * ns + i + 1) * r8, last8), c)),
                  pl.BlockSpec((DN_CONV, cb), lambda b, i, c: (0, c))],
        out_specs=pl.BlockSpec((ts, cb), lambda b, i, c: (b * ns + i, c)),
        out_shape=jax.ShapeDtypeStruct((batch * seq, width), BF16),
        scratch_shapes=[pltpu.VMEM((ts + 2 * SUBLANES, cb), F32)],
        compiler_params=_params(("parallel", "parallel", "parallel")),
        name="dn_conv",
    )(p2, p2, p2, conv_w)


def _dn_gate_body(x_ref, alog_ref, dtb_ref, o_ref, *, heads):
    x = x_ref[...]
    z = x + dtb_ref[...]
    softplus = jnp.maximum(z, 0.0) + jnp.log1p(jnp.exp(-jnp.abs(z)))
    g = -jnp.exp(alog_ref[...]) * softplus
    lane = lax.broadcasted_iota(jnp.int32, x.shape, 1)
    zt = jnp.where(lane < LANES - 2 * heads, g, jax.nn.sigmoid(x)).T
    o_ref[0, 2 * heads:] = zt[LANES - 2 * heads:]
    pos = lax.broadcasted_iota(jnp.int32, (2 * heads, LANES), 1) % CHUNK
    fwd = lax.broadcasted_iota(jnp.int32, (2 * heads, LANES), 0) < heads
    for grp in range(zt.shape[1] // LANES):
        cols = slice(grp * LANES, (grp + 1) * LANES)
        gt = zt[LANES - 4 * heads:LANES - 2 * heads, cols]
        pre, suf = gt, gt
        s = 1
        while s < CHUNK:
            pre = pre + jnp.where(pos >= s, pltpu.roll(pre, s, 1), 0.0)
            suf = suf + jnp.where(pos < CHUNK - s, pltpu.roll(suf, LANES - s, 1), 0.0)
            s *= 2
        o_ref[0, :2 * heads, cols] = jnp.where(fwd, pre, suf)


def _dn_gates(p1, a_log, dt_bias, *, batch, seq, heads, col_block):
    ts = _tile(seq, 512)
    ns = seq // ts
    lo, hi = LANES - 4 * heads, LANES - 2 * heads
    alog = jnp.zeros((1, LANES), F32).at[0, lo:hi].set(a_log.reshape(-1))
    dtb = jnp.zeros((1, LANES), F32).at[0, lo:hi].set(dt_bias.reshape(-1))
    return pl.pallas_call(
        functools.partial(_dn_gate_body, heads=heads),
        grid=(batch, ns),
        in_specs=[pl.BlockSpec((ts, LANES), lambda b, i: (b * ns + i, col_block)),
                  pl.BlockSpec((1, LANES), lambda b, i: (0, 0)),
                  pl.BlockSpec((1, LANES), lambda b, i: (0, 0))],
        out_specs=pl.BlockSpec((1, 4 * heads, ts), lambda b, i: (b, 0, i)),
        out_shape=jax.ShapeDtypeStruct((batch, 4 * heads, seq), F32),
        compiler_params=_params(("parallel", "parallel")),
        name="dn_gates",
    )(p1, alog, dtb)


def _split_dot(a, b):
    ah, bh = a.astype(BF16), b.astype(BF16)
    al = (a - ah.astype(F32)).astype(BF16)
    bl = (b - bh.astype(F32)).astype(BF16)
    return _dot(ah, bh) + _dot(ah, bl) + _dot(al, bh)


def _unit_triangular_inverses(ms, eye):
    a = [eye + m for m in ms]
    ab = [x.astype(BF16) for x in a]
    t = [eye - m for m in ms]
    for _ in range(int(math.log2(CHUNK)) - 2):
        r = [2.0 * eye - _dot(x, y.astype(BF16)) for x, y in zip(ab, t)]
        t = [_dot(x.astype(BF16), y.astype(BF16)) for x, y in zip(t, r)]
    r = [eye - _split_dot(x, y) for x, y in zip(a, t)]
    return [x + _dot(x.astype(BF16), y.astype(BF16)) for x, y in zip(t, r)]


def _dn_local_body(q_ref, k_ref, v_ref, grow_ref, glast_ref,
                   u_ref, w_ref, qd_ref, kd_ref, qk_ref, *, chunks):
    ii = lax.broadcasted_iota(jnp.int32, (CHUNK, CHUNK), 0)
    jj = lax.broadcasted_iota(jnp.int32, (CHUNK, CHUNK), 1)
    eye = (ii == jj).astype(F32)
    rows = [slice(c * CHUNK, (c + 1) * CHUNK) for c in range(chunks)]
    q = [q_ref[r, :] for r in rows]
    k = [k_ref[r, :] for r in rows]
    v = [v_ref[r, :] for r in rows]
    qk_kk = [_dot_nt(jnp.concatenate([x, y], axis=0), y) for x, y in zip(q, k)]

    chains = [(c, d) for c in range(chunks) for d in range(2)]
    g_row, b_row, g_col, g_last, decay, ms = [], [], [], [], [], []
    for c, d in chains:
        incl = (jj <= ii) if d == 0 else (jj >= ii)
        strict = (jj < ii) if d == 0 else (jj > ii)
        r = jnp.broadcast_to(grow_ref[0, d, 0, c:c + 1, :], (LANES, LANES))
        rt = r.T
        gc, bc = rt[:CHUNK], rt[CHUNK:]
        gr = r[:CHUNK, :CHUNK]
        dec = jnp.exp(jnp.where(incl, gc[:, :CHUNK] - gr, -jnp.inf))
        ms.append(jnp.where(strict, qk_kk[c][CHUNK:] * bc[:, :CHUNK] * dec, 0.0))
        g_row.append(gr)
        b_row.append(r[:CHUNK, CHUNK:])
        g_col.append(gc)
        g_last.append(jnp.broadcast_to(glast_ref[0, d, 0, c:c + 1, :], (CHUNK, LANES)))
        decay.append(dec)

    t_beta = [t * b for t, b in zip(_unit_triangular_inverses(ms, eye), b_row)]
    us = [_dot(t.astype(BF16), v[c]) for t, (c, d) in zip(t_beta, chains)]
    ws = [_dot((t * jnp.exp(g)).astype(BF16), k[c]) for t, g, (c, d) in zip(t_beta, g_row, chains)]
    for i, (c, d) in enumerate(chains):
        u_ref[0, d, 0, rows[c], :] = us[i]
        w_ref[0, d, 0, rows[c], :] = ws[i].astype(BF16)
        qk_ref[0, d, 0, rows[c], :] = (qk_kk[c][:CHUNK] * decay[i]).astype(BF16)
        qd_ref[0, d, 0, rows[c], :] = (q[c].astype(F32) * jnp.exp(g_col[i])).astype(BF16)
        kd_ref[0, d, 0, rows[c], :] = (k[c].astype(F32) * jnp.exp(g_last[i] - g_col[i])).astype(BF16)


def _dn_local(qkv, grow, glast, *, batch, seq, heads):
    nc = seq // CHUNK
    cbk = _tile(nc, 16)
    L = cbk * CHUNK
    nb = nc // cbk
    tok = lambda off: pl.BlockSpec((L, DN_K), lambda b, h, t: (b * nb + t, off + h))
    gate = pl.BlockSpec((1, 2, 1, cbk, LANES), lambda b, h, t: (b, 0, h, t, 0))
    out = lambda width: pl.BlockSpec((1, 2, 1, L, width), lambda b, h, t: (b, 0, h, t, 0))
    shp = lambda width, dt: jax.ShapeDtypeStruct((batch, 2, heads, seq, width), dt)
    return pl.pallas_call(
        functools.partial(_dn_local_body, chunks=cbk),
        grid=(batch, heads, nb),
        in_specs=[tok(0), tok(heads), tok(2 * heads), gate, gate],
        out_specs=[out(DN_V), out(DN_K), out(DN_K), out(DN_K), out(CHUNK)],
        out_shape=[shp(DN_V, F32), shp(DN_K, BF16), shp(DN_K, BF16), shp(DN_K, BF16), shp(CHUNK, BF16)],
        compiler_params=_params(("parallel", "parallel", "parallel")),
        name="dn_local",
    )(qkv, qkv, qkv, grow, glast)


def _dn_scan_body(*refs, chunks, hb):
    ins, (of_ref, ob_ref, st_ref) = refs[:12], refs[12:]
    t = pl.program_id(2)

    @pl.when(t == 0)
    def _():
        st_ref[...] = jnp.zeros_like(st_ref)

    chains = [(d, j) for d in range(2) for j in range(hb)]
    states = [st_ref[d, j] for d, j in chains]
    for ci in range(chunks):
        blk = []
        for d, j in chains:
            u_ref, w_ref, qd_ref, kd_ref, qk_ref, gl_ref = ins[6 * d:6 * d + 6]
            c = ci if d == 0 else chunks - 1 - ci
            rows = slice(c * CHUNK, (c + 1) * CHUNK)
            blk.append((u_ref[0, 0, j, rows, :], w_ref[0, 0, j, rows, :], qd_ref[0, 0, j, rows, :],
                        kd_ref[0, 0, j, rows, :], qk_ref[0, 0, j, rows, :],
                        jnp.exp(gl_ref[0, 0, j, c:c + 1, :]), rows))
        sb = [s.astype(BF16) for s in states]
        w_s = [_dot(b[1], s) for b, s in zip(blk, sb)]
        q_s = [_dot(b[2], s) for b, s in zip(blk, sb)]
        vb = [(b[0] - x).astype(BF16) for b, x in zip(blk, w_s)]
        outs = [x + _dot(b[4], y) for b, x, y in zip(blk, q_s, vb)]
        states = [s * b[5] + _dot_tn(b[3], y) for b, s, y in zip(blk, states, vb)]
        for (d, j), b, o in zip(chains, blk, outs):
            o_ref = of_ref if d == 0 else ob_ref
            o_ref[b[6], j * DN_V:(j + 1) * DN_V] = o
    for (d, j), s in zip(chains, states):
        st_ref[d, j] = s


def _dn_scan(u, w, qd, kd, qk, glast, *, batch, seq, heads):
    nc = seq // CHUNK
    cbk = _tile(nc, 8)
    L = cbk * CHUNK
    nb = nc // cbk
    hb = _tile(heads, 8)

    def spec(d, rows, width):
        if d == 0:
            return pl.BlockSpec((1, 1, hb, rows, width), lambda b, g, t: (b, 0, g, t, 0))
        return pl.BlockSpec((1, 1, hb, rows, width), lambda b, g, t: (b, 1, g, nb - 1 - t, 0))

    in_specs, args = [], []
    for d in range(2):
        in_specs += [spec(d, L, DN_V), spec(d, L, DN_K), spec(d, L, DN_K), spec(d, L, DN_K),
                     spec(d, L, CHUNK), spec(d, cbk, LANES)]
        args += [u, w, qd, kd, qk, glast]
    out_shape = jax.ShapeDtypeStruct((batch * seq, heads * DN_V), F32)
    return pl.pallas_call(
        functools.partial(_dn_scan_body, chunks=cbk, hb=hb),
        grid=(batch, heads // hb, nb),
        in_specs=in_specs,
        out_specs=[pl.BlockSpec((L, hb * DN_V), lambda b, g, t: (b * nb + t, g)),
                   pl.BlockSpec((L, hb * DN_V), lambda b, g, t: (b * nb + nb - 1 - t, g))],
        out_shape=[out_shape, out_shape],
        scratch_shapes=[pltpu.VMEM((2, hb, DN_K, DN_V), F32)],
        compiler_params=_params(("parallel", "parallel", "arbitrary")),
        name="dn_scan",
    )(*args)


def _dn_out_body(of_ref, ob_ref, z_ref, g_ref, o_ref):
    o = of_ref[...] + ob_ref[...]
    z = z_ref[...]
    g = g_ref[...]
    for h in range(o.shape[1] // DN_V):
        sl = slice(h * DN_V, (h + 1) * DN_V)
        oh, zh = o[:, sl], z[:, sl]
        y = oh * lax.rsqrt(jnp.mean(oh * oh, axis=-1, keepdims=True) + EPS) * g
        o_ref[:, sl] = (y * (zh * jax.nn.sigmoid(zh))).astype(o_ref.dtype)


def _dn_out(o_f, o_b, p2, norm_g, *, z_col0):
    M, W = o_f.shape
    tm, tw = _tile(M, 512), _tile(W, 512)
    zj = z_col0 // tw
    return pl.pallas_call(
        _dn_out_body,
        grid=(M // tm, W // tw),
        in_specs=[pl.BlockSpec((tm, tw), lambda i, j: (i, j)),
                  pl.BlockSpec((tm, tw), lambda i, j: (i, j)),
                  pl.BlockSpec((tm, tw), lambda i, j: (i, zj + j)),
                  pl.BlockSpec((1, DN_V), lambda i, j: (0, 0))],
        out_specs=pl.BlockSpec((tm, tw), lambda i, j: (i, j)),
        out_shape=jax.ShapeDtypeStruct((M, W), BF16),
        compiler_params=_params(("parallel", "parallel")),
        name="dn_out",
    )(o_f, o_b, p2, norm_g.reshape(1, DN_V))


def _rope_table(positions):
    half = ROPE // 2
    inv_freq = ROPE_THETA ** (-jnp.arange(half, dtype=F32) / half)
    ang = positions.astype(F32)[..., None] * inv_freq
    cos, sin = jnp.cos(ang), jnp.sin(ang)
    return jnp.concatenate([cos, cos, sin, sin], axis=-1).reshape(-1, 4 * half)


def _mixer(u, tab, w_in, q_norm_g, kv_norm_g, w_uq, w_ukv, w_branch_a, conv_w, a_log, dt_bias,
           o_norm_g, w_branch_b, w_out, *, batch, seq, stream, gate, alpha):
    d_model = u.shape[1]
    q_rank, kv_rank = q_norm_g.shape[0], kv_norm_g.shape[0]
    mla_heads = w_uq.shape[1] // MLA_QK
    dn_heads = a_log.shape[1]
    dn_qk_w, dn_v_w = dn_heads * DN_K, dn_heads * DN_V
    c_qkv = q_rank + kv_rank + ROPE
    c_z = c_qkv + 2 * dn_qk_w + dn_v_w
    c_a = c_z + dn_v_w
    c_gate = c_a + 4 * dn_heads
    assert ROPE + 4 * dn_heads == LANES and c_qkv - ROPE == (q_rank + kv_rank)
    w1 = jnp.concatenate([w_in[:, :c_qkv], w_in[:, c_a:c_gate]], axis=1).astype(BF16)
    w2 = jnp.concatenate([w_in[:, c_qkv:c_a], w_in[:, c_gate:]], axis=1).astype(BF16)
    p1 = _matmul(u, w1, F32, tm=512, tn=w1.shape[1])
    p2 = _matmul(u, w2, F32)

    wq = w_uq.reshape(q_rank, mla_heads, MLA_QK)
    r1, r2 = wq[..., NOPE:NOPE + ROPE // 2], wq[..., NOPE + ROPE // 2:]
    wq = jnp.concatenate([wq, -r2, r1], axis=-1).transpose(1, 0, 2).astype(BF16)
    wkv = w_ukv.reshape(kv_rank, mla_heads, NOPE + MLA_V).transpose(1, 0, 2).astype(BF16)
    q, k, v = _mla_prep(p1, tab, q_norm_g, kv_norm_g, wq, wkv, batch=batch, seq=seq)
    o_a = _flash(q, k, v).reshape(batch * seq, mla_heads * MLA_V)

    qkv = _dn_conv(p2, conv_w, batch=batch, seq=seq, heads=dn_heads)
    gates = _dn_gates(p1, a_log, dt_bias, batch=batch, seq=seq, heads=dn_heads,
                      col_block=(q_rank + kv_rank) // LANES)
    nc = seq // CHUNK
    gates = gates.reshape(batch, 2, 2, dn_heads, nc, CHUNK)
    grow = gates.transpose(0, 2, 3, 4, 1, 5).reshape(batch, 2, dn_heads, nc, 2 * CHUNK)
    g_cum = gates[:, 0]
    glast = jnp.stack([g_cum[:, 0, :, :, CHUNK - 1], g_cum[:, 1, :, :, 0]], axis=1)
    glast = jnp.broadcast_to(glast[..., None], glast.shape + (LANES,))
    u_, w_, qd, kd, qk = _dn_local(qkv, grow, glast, batch=batch, seq=seq, heads=dn_heads)
    o_f, o_b = _dn_scan(u_, w_, qd, kd, qk, glast, batch=batch, seq=seq, heads=dn_heads)
    o_bn = _dn_out(o_f, o_b, p2, o_norm_g, z_col0=2 * dn_qk_w + dn_v_w)

    merged = _merge(o_a, o_bn, w_branch_a.astype(BF16), w_branch_b.astype(BF16), p2, d_model,
                    gate_col0=2 * dn_qk_w + 2 * dn_v_w)
    return _matmul(merged, w_out.astype(BF16), F32, tn=512, stream=stream, gate=gate, seq=seq,
                   residual=(alpha, 1.0))


def _ffn(h, w_gate, w_up, w_down, layer, *, stream, gate, seq, alpha):
    a = _gateup(h, w_gate, w_up, layer)
    return _matmul(a, w_down[layer].astype(BF16), F32, tm=512, tn=512, tk=w_down.shape[1],
                   stream=stream, gate=gate, seq=seq, residual=(alpha, 0.5))


def kernel(x, c, positions, ln_in_g, ln_in_b, w_ada, b_ada, ada_table, ffn1_w_gate, ffn1_w_up, ffn1_w_down, w_in, mla_q_norm_g, mla_kv_norm_g, mla_w_uq, mla_w_ukv, w_branch_a, dn_conv_w, dn_a_log, dn_dt_bias, dn_norm_g, w_branch_b, w_out, ffn2_w_gate, ffn2_w_up, ffn2_w_down, post_ln_g, post_ln_b):
    B, S, D = x.shape
    depth = ada_table.shape[0]
    alpha = (2.0 * depth) ** 0.25
    cond = _adaln(c, w_ada, b_ada).reshape(B, N_SUB, 3, D)
    mod = cond[None] + ada_table[:, None]
    tab = _rope_table(positions)
    post = functools.partial(_post, seq=S)

    xs, g, b = x.reshape(B * S, D), ln_in_g, ln_in_b
    h, stats = post(xs, g, b, mod[0, :, 0, 0], mod[0, :, 0, 1])
    for l in range(depth):
        xs = _ffn(h, ffn1_w_gate, ffn1_w_up, ffn1_w_down, l, stream=(xs, stats, g, b), gate=mod[l, :, 0, 2],
                  seq=S, alpha=alpha)
        g, b = post_ln_g[l, 0], post_ln_b[l, 0]
        h, stats = post(xs, g, b, mod[l, :, 1, 0], mod[l, :, 1, 1])
        xs = _mixer(h, tab, w_in[l], mla_q_norm_g[l], mla_kv_norm_g[l], mla_w_uq[l], mla_w_ukv[l],
                    w_branch_a[l], dn_conv_w[l], dn_a_log[l], dn_dt_bias[l], dn_norm_g[l],
                    w_branch_b[l], w_out[l], batch=B, seq=S, stream=(xs, stats, g, b), gate=mod[l, :, 1, 2],
                    alpha=alpha)
        g, b = post_ln_g[l, 1], post_ln_b[l, 1]
        h, stats = post(xs, g, b, mod[l, :, 2, 0], mod[l, :, 2, 1])
        xs = _ffn(h, ffn2_w_gate, ffn2_w_up, ffn2_w_down, l, stream=(xs, stats, g, b), gate=mod[l, :, 2, 2],
                  seq=S, alpha=alpha)
        g, b = post_ln_g[l, 2], post_ln_b[l, 2]
        if l + 1 < depth:
            h, stats = post(xs, g, b, mod[l + 1, :, 0, 0], mod[l + 1, :, 0, 1])
    return post(xs, g, b, None, None).reshape(B, S, D)
```

```python
import functools
import math

import jax
import jax.numpy as jnp
from jax import lax
from jax.experimental import pallas as pl
from jax.experimental.pallas import tpu as pltpu

F32 = jnp.float32
BF16 = jnp.bfloat16

NOPE = 128
ROPE = 64
MLA_V = 128
MLA_QK = NOPE + ROPE
DN_K = 128
DN_V = 128
DN_CONV = 5
CHUNK = 64
N_SUB = 3
EPS = 1e-6
ROPE_THETA = 10000.0
LOG2E = math.log2(math.e)

LANES = 128
SUBLANES = 8
V7X_VMEM_LIMIT_BYTES = 56 * 1024 * 1024


def _tile(dim, pref):
    t = min(dim, pref)
    assert dim % t == 0, (dim, pref)
    return t


def _params(sem):
    return pltpu.CompilerParams(dimension_semantics=sem, vmem_limit_bytes=V7X_VMEM_LIMIT_BYTES)


def _dot(a, b):
    return jnp.dot(a, b, preferred_element_type=F32)


def _dot_nt(a, b):
    return lax.dot_general(a, b, (((1,), (1,)), ((), ())), preferred_element_type=F32)


def _dot_tn(a, b):
    return lax.dot_general(a, b, (((0,), (0,)), ((), ())), preferred_element_type=F32)


def _adaln_body(c_ref, w_ref, b_ref, o_ref):
    c = c_ref[...]
    h = (c * jax.nn.sigmoid(c)).astype(BF16)
    o_ref[...] = _dot(h, w_ref[...].astype(BF16)) + b_ref[...]


def _adaln(c, w_ada, b_ada):
    B, D = c.shape
    N = w_ada.shape[1]
    rows = max(SUBLANES, B)
    c_pad = jnp.zeros((rows, D), F32).at[:B].set(c)
    tn = _tile(N, 512)
    out = pl.pallas_call(
        _adaln_body,
        grid=(N // tn,),
        in_specs=[pl.BlockSpec((rows, D), lambda j: (0, 0)),
                  pl.BlockSpec((D, tn), lambda j: (0, j)),
                  pl.BlockSpec((1, tn), lambda j: (0, j))],
        out_specs=pl.BlockSpec((rows, tn), lambda j: (0, j)),
        out_shape=jax.ShapeDtypeStruct((rows, N), F32),
        compiler_params=_params(("parallel",)),
        name="adaln",
    )(c_pad, w_ada, b_ada.reshape(1, N))
    return out[:B]


def _post_body(*refs, has_mod):
    it = iter(refs)
    x_ref, g_ref, b_ref = next(it), next(it), next(it)
    if has_mod:
        sh_ref, sc_ref, h_ref, st_ref = next(it), next(it), next(it), next(it)
    else:
        xo_ref = next(it)
    x = x_ref[...]
    mu = jnp.mean(x, axis=-1, keepdims=True)
    xc = x - mu
    var = jnp.mean(xc * xc, axis=-1, keepdims=True)
    rstd = lax.rsqrt(var + EPS)
    y = xc * rstd * g_ref[...] + b_ref[...]
    if has_mod:
        h_ref[...] = (y * (1.0 + sc_ref[0]) + sh_ref[0]).astype(BF16)
        st_ref[:, :LANES] = jnp.broadcast_to(mu, (mu.shape[0], LANES))
        st_ref[:, LANES:] = jnp.broadcast_to(rstd, (rstd.shape[0], LANES))
    else:
        xo_ref[...] = y


def _post(x, g, b, shift, scale, *, seq):
    M, D = x.shape
    has_mod = shift is not None
    tm = _tile(seq, 256)
    per_b = seq // tm
    row = pl.BlockSpec((tm, D), lambda i: (i, 0))
    vec = pl.BlockSpec((1, D), lambda i: (0, 0))
    bvec = pl.BlockSpec((1, 1, D), lambda i: (i // per_b, 0, 0))
    args = [x, g.reshape(1, D), b.reshape(1, D)]
    specs = [row, vec, vec]
    if has_mod:
        args += [shift.reshape(-1, 1, D), scale.reshape(-1, 1, D)]
        specs += [bvec, bvec]
        out_shape = [jax.ShapeDtypeStruct((M, D), BF16), jax.ShapeDtypeStruct((M, 2 * LANES), F32)]
        out_specs = [row, pl.BlockSpec((tm, 2 * LANES), lambda i: (i, 0))]
    else:
        out_shape = [jax.ShapeDtypeStruct((M, D), F32)]
        out_specs = [row]
    outs = pl.pallas_call(
        functools.partial(_post_body, has_mod=has_mod),
        grid=(M // tm,),
        in_specs=specs,
        out_specs=out_specs,
        out_shape=out_shape,
        compiler_params=_params(("parallel",)),
        name="post_ln",
    )(*args)
    return outs if has_mod else outs[0]


def _mm_body(*refs, nk, residual):
    a_ref, w_ref = refs[:2]
    refs = refs[2:]
    if residual is not None:
        (x_ref, st_ref, lg_ref, lb_ref, gate_ref), refs = refs[:5], refs[5:]
    o_ref = refs[0]

    def finish(acc):
        if residual is None:
            o_ref[...] = acc.astype(o_ref.dtype)
            return
        alpha, r = residual
        mu, rstd = st_ref[:, :LANES], st_ref[:, LANES:]
        for s in range(acc.shape[1] // LANES):
            sl = slice(s * LANES, (s + 1) * LANES)
            xn = (x_ref[:, sl] - mu) * rstd * lg_ref[:, sl] + lb_ref[:, sl]
            o_ref[:, sl] = alpha * xn + (r * gate_ref[0][:, sl]) * acc[:, sl]

    if nk == 1:
        finish(_dot(a_ref[...], w_ref[...]))
        return
    acc_ref = refs[1]
    k = pl.program_id(2)

    @pl.when(k == 0)
    def _():
        acc_ref[...] = jnp.zeros_like(acc_ref)

    acc_ref[...] += _dot(a_ref[...], w_ref[...])

    @pl.when(k == nk - 1)
    def _():
        finish(acc_ref[...])


def _matmul(a, w, out_dtype, *, tm=1024, tn=1024, tk=4096, stream=None, gate=None, seq=None, residual=None,
            columns_outer=False):
    M, K = a.shape
    N = w.shape[1]
    tm, tn, tk = _tile(M if residual is None else seq, tm), _tile(N, tn), _tile(K, tk)
    nk = K // tk
    if columns_outer:
        assert nk == 1
        grid = (N // tn, M // tm, nk)
        at = lambda f: (lambda j, i, k: f(i, j, k))
        w_spec = pl.BlockSpec((tk, tn), at(lambda i, j, k: (k, j)), pipeline_mode=pl.Buffered(1))
    else:
        grid = (M // tm, N // tn, nk)
        at = lambda f: f
        w_spec = pl.BlockSpec((tk, tn), lambda i, j, k: (k, j))
    args = [a, w]
    in_specs = [pl.BlockSpec((tm, tk), at(lambda i, j, k: (i, k))), w_spec]
    if residual is not None:
        per_b = seq // tm
        x, stats, g, b = stream
        args += [x, stats, g.reshape(1, N), b.reshape(1, N), gate.reshape(-1, 1, N)]
        in_specs += [pl.BlockSpec((tm, tn), at(lambda i, j, k: (i, j))),
                     pl.BlockSpec((tm, 2 * LANES), at(lambda i, j, k: (i, 0))),
                     pl.BlockSpec((1, tn), at(lambda i, j, k: (0, j))),
                     pl.BlockSpec((1, tn), at(lambda i, j, k: (0, j))),
                     pl.BlockSpec((1, 1, tn), at(lambda i, j, k: (i // per_b, 0, j)))]
    return pl.pallas_call(
        functools.partial(_mm_body, nk=nk, residual=residual),
        grid=grid,
        in_specs=in_specs,
        out_specs=pl.BlockSpec((tm, tn), at(lambda i, j, k: (i, j))),
        out_shape=jax.ShapeDtypeStruct((M, N), out_dtype),
        scratch_shapes=[pltpu.VMEM((tm, tn), F32)] if nk > 1 else [],
        compiler_params=_params(("parallel", "parallel", "arbitrary")),
        name="matmul",
    )(*args)


def _gateup_body(a_ref, wg_ref, wu_ref, o_ref, wgb_ref, wub_ref):
    @pl.when(pl.program_id(1) == 0)
    def _():
        wgb_ref[...] = wg_ref[0].astype(BF16)
        wub_ref[...] = wu_ref[0].astype(BF16)

    a = a_ref[...]
    g = _dot(a, wgb_ref[...])
    u = _dot(a, wub_ref[...])
    o_ref[...] = (g * jax.nn.sigmoid(g) * u).astype(o_ref.dtype)


def _gateup(a, wg, wu, layer):
    M, K = a.shape
    N = wg.shape[2]
    tm, tn = _tile(M, 1024), _tile(N, 512)
    weight = pl.BlockSpec((1, K, tn), lambda j, i: (layer, 0, j), pipeline_mode=pl.Buffered(1))
    return pl.pallas_call(
        _gateup_body,
        grid=(N // tn, M // tm),
        in_specs=[pl.BlockSpec((tm, K), lambda j, i: (i, 0)), weight, weight],
        out_specs=pl.BlockSpec((tm, tn), lambda j, i: (i, j)),
        out_shape=jax.ShapeDtypeStruct((M, N), BF16),
        scratch_shapes=[pltpu.VMEM((K, tn), BF16), pltpu.VMEM((K, tn), BF16)],
        compiler_params=_params(("parallel", "arbitrary")),
        name="ffn_gate_up",
    )(a, wg, wu)


def _merge_body(oa_ref, ob_ref, wa_ref, wb_ref, ga_ref, gb_ref, o_ref):
    ya = _dot(oa_ref[...], wa_ref[...])
    yb = _dot(ob_ref[...], wb_ref[...])
    o_ref[...] = (jax.nn.sigmoid(ga_ref[...]) * ya + jax.nn.sigmoid(gb_ref[...]) * yb).astype(o_ref.dtype)


def _merge(oa, ob, wa, wb, p2, d_model, gate_col0):
    M, Ka = oa.shape
    Kb = ob.shape[1]
    tm, tn = _tile(M, 1024), _tile(d_model, 512)
    ja = gate_col0 // tn
    jb = (gate_col0 + d_model) // tn
    return pl.pallas_call(
        _merge_body,
        grid=(M // tm, d_model // tn),
        in_specs=[pl.BlockSpec((tm, Ka), lambda i, j: (i, 0)),
                  pl.BlockSpec((tm, Kb), lambda i, j: (i, 0)),
                  pl.BlockSpec((Ka, tn), lambda i, j: (0, j)),
                  pl.BlockSpec((Kb, tn), lambda i, j: (0, j)),
                  pl.BlockSpec((tm, tn), lambda i, j: (i, ja + j)),
                  pl.BlockSpec((tm, tn), lambda i, j: (i, jb + j))],
        out_specs=pl.BlockSpec((tm, tn), lambda i, j: (i, j)),
        out_shape=jax.ShapeDtypeStruct((M, d_model), BF16),
        compiler_params=_params(("parallel", "parallel")),
        name="branch_merge",
    )(oa, ob, wa, wb, p2, p2)


def _rms(x, g):
    return x * lax.rsqrt(jnp.mean(x * x, axis=-1, keepdims=True) + EPS) * g


def _mla_prep_body(ql_ref, kvl_ref, kr_ref, tab_ref, gq_ref, gkv_ref, wq_ref, wkv_ref,
                   q_ref, k_ref, v_ref, *, heads, q_scale):
    qn = _rms(ql_ref[...], gq_ref[...]).astype(BF16)
    kvn = _rms(kvl_ref[...], gkv_ref[...]).astype(BF16)
    tab = tab_ref[...]
    tab_sin = pltpu.roll(tab, 64, 1)
    x = kr_ref[...]
    lane = lax.broadcasted_iota(jnp.int32, x.shape, 1)
    x_rot = jnp.where(lane < ROPE // 2, -pltpu.roll(x, LANES - ROPE // 2, 1), pltpu.roll(x, ROPE // 2, 1))
    k_rope_t = (x * tab + x_rot * tab_sin).T[:ROPE].astype(BF16)
    for h in range(heads):
        r = _dot(qn, wq_ref[h])
        p = r[:, NOPE:] * tab
        roped = p + pltpu.roll(p, 64, 1)
        q_ref[0, h, :, :NOPE] = (r[:, :NOPE] * q_scale).astype(BF16)
        q_ref[0, h, :, NOPE:] = (roped[:, :ROPE] * q_scale).astype(BF16)
        kv = _dot(kvn, wkv_ref[h])
        k_ref[0, h, :NOPE, :] = kv[:, :NOPE].T.astype(BF16)
        k_ref[0, h, NOPE:, :] = k_rope_t
        v_ref[0, h] = kv[:, NOPE:].astype(BF16)


def _mla_prep(p1, tab, gq, gkv, wq, wkv, *, batch, seq):
    heads, q_rank, _ = wq.shape
    kv_rank = wkv.shape[1]
    ts = _tile(seq, 512)
    ns = seq // ts
    row = lambda b, i: b * ns + i
    return pl.pallas_call(
        functools.partial(_mla_prep_body, heads=heads, q_scale=MLA_QK ** -0.5 * LOG2E),
        grid=(batch, ns),
        in_specs=[pl.BlockSpec((ts, q_rank), lambda b, i: (row(b, i), 0)),
                  pl.BlockSpec((ts, kv_rank), lambda b, i: (row(b, i), q_rank // kv_rank)),
                  pl.BlockSpec((ts, LANES), lambda b, i: (row(b, i), (q_rank + kv_rank) // LANES)),
                  pl.BlockSpec((ts, LANES), lambda b, i: (row(b, i), 0)),
                  pl.BlockSpec((1, q_rank), lambda b, i: (0, 0)),
                  pl.BlockSpec((1, kv_rank), lambda b, i: (0, 0)),
                  pl.BlockSpec((heads, q_rank, 2 * LANES), lambda b, i: (0, 0, 0)),
                  pl.BlockSpec((heads, kv_rank, 2 * LANES), lambda b, i: (0, 0, 0))],
        out_specs=[pl.BlockSpec((1, heads, ts, MLA_QK), lambda b, i: (b, 0, i, 0)),
                   pl.BlockSpec((1, heads, MLA_QK, ts), lambda b, i: (b, 0, 0, i)),
                   pl.BlockSpec((1, heads, ts, MLA_V), lambda b, i: (b, 0, i, 0))],
        out_shape=[jax.ShapeDtypeStruct((batch, heads, seq, MLA_QK), BF16),
                   jax.ShapeDtypeStruct((batch, heads, MLA_QK, seq), BF16),
                   jax.ShapeDtypeStruct((batch, heads, seq, MLA_V), BF16)],
        compiler_params=_params(("parallel", "parallel")),
        name="mla_prep",
    )(p1, p1, p1, tab, gq.reshape(1, -1), gkv.reshape(1, -1), wq, wkv)


def _flash_body(q_ref, kt_ref, v_ref, o_ref, s_ref, m_ref, l_ref, acc_ref, *, tk, nk):
    m_ref[...] = jnp.full_like(m_ref, -jnp.inf)
    l_ref[...] = jnp.zeros_like(l_ref)
    acc_ref[...] = jnp.zeros_like(acc_ref)
    q = q_ref[0, 0]

    def key_rows(j):
        return slice(j * tk, (j + 1) * tk)

    def scores(j):
        return _dot(q, kt_ref[0, 0, :, key_rows(j)])

    s_ref[0] = scores(0)
    for j in range(nk):
        s = s_ref[j % 2]
        if j + 1 < nk:
            s_ref[(j + 1) % 2] = scores(j + 1)
        m_prev = m_ref[...]
        m_new = jnp.maximum(m_prev, jnp.max(s, axis=1, keepdims=True))
        alpha = jnp.exp2(m_prev - m_new)
        p = jnp.exp2(s - m_new[:, :1])
        l_ref[...] = alpha * l_ref[...] + jnp.sum(p, axis=1, keepdims=True)
        acc_ref[...] = alpha * acc_ref[...] + _dot(p.astype(BF16), v_ref[0, 0, key_rows(j), :])
        m_ref[...] = m_new
    o_ref[0] = (acc_ref[...] / l_ref[...]).astype(o_ref.dtype)


def _flash(q, k, v):
    B, H, S, _ = q.shape
    tq, tk = _tile(S, 1024), _tile(S, 2048)
    return pl.pallas_call(
        functools.partial(_flash_body, tk=tk, nk=S // tk),
        grid=(B, H, S // tq),
        in_specs=[pl.BlockSpec((1, 1, tq, MLA_QK), lambda b, h, i: (b, h, i, 0)),
                  pl.BlockSpec((1, 1, MLA_QK, S), lambda b, h, i: (b, h, 0, 0)),
                  pl.BlockSpec((1, 1, S, MLA_V), lambda b, h, i: (b, h, 0, 0))],
        out_specs=pl.BlockSpec((1, tq, MLA_V), lambda b, h, i: (b, i, h)),
        out_shape=jax.ShapeDtypeStruct((B, S, H * MLA_V), BF16),
        scratch_shapes=[pltpu.VMEM((2, tq, tk), F32), pltpu.VMEM((tq, LANES), F32),
                        pltpu.VMEM((tq, LANES), F32), pltpu.VMEM((tq, MLA_V), F32)],
        compiler_params=_params(("parallel", "parallel", "arbitrary")),
        name="mla_flash",
    )(q, k, v)


def _dn_conv_body(cur_ref, prev_ref, next_ref, w_ref, o_ref, buf_ref, *, ts, ns, qk_blocks):
    i = pl.program_id(1)
    c = pl.program_id(2)
    pad = DN_CONV // 2
    buf_ref[0:SUBLANES] = jnp.where(i == 0, 0.0, prev_ref[...])
    buf_ref[SUBLANES:SUBLANES + ts] = cur_ref[...]
    buf_ref[SUBLANES + ts:] = jnp.where(i == ns - 1, 0.0, next_ref[...])
    w = w_ref[...]
    full = buf_ref[...]
    rows = full.shape[0]
    acc = w[pad:pad + 1] * full[SUBLANES:SUBLANES + ts]
    for t in range(DN_CONV):
        if t != pad:
            acc = acc + w[t:t + 1] * pltpu.roll(full, (pad - t) % rows, 0)[SUBLANES:SUBLANES + ts]
    y = acc * jax.nn.sigmoid(acc)
    kind = c // qk_blocks
    for g in range(y.shape[1] // DN_K):
        yg = y[:, g * DN_K:(g + 1) * DN_K]
        inv = lax.rsqrt(jnp.sum(yg * yg, axis=-1, keepdims=True) + EPS)
        inv = jnp.where(kind < 2, inv, 1.0) * jnp.where(kind == 0, DN_K ** -0.5, 1.0)
        o_ref[:, g * DN_K:(g + 1) * DN_K] = (yg * inv).astype(o_ref.dtype)


def _dn_conv(p2, conv_w, *, batch, seq, heads):
    width = 3 * heads * DN_K
    ts = _tile(seq, 512)
    ns = seq // ts
    cb = 4 * DN_K
    qk_blocks = heads * DN_K // cb
    r8 = ts // SUBLANES
    last8 = batch * seq // SUBLANES - 1
    return pl.pallas_call(
        functools.partial(_dn_conv_body, ts=ts, ns=ns, qk_blocks=qk_blocks),
        grid=(batch, ns, width // cb),
        in_specs=[pl.BlockSpec((ts, cb), lambda b, i, c: (b * ns + i, c)),
                  pl.BlockSpec((SUBLANES, cb), lambda b, i, c: (jnp.maximum((b * ns + i) * r8 - 1, 0), c)),
                  pl.BlockSpec((SUBLANES, cb), lambda b, i, c: (jnp.minimum((b * ns + i + 1) * r8, last8), c)),
                  pl.BlockSpec((DN_CONV, cb), lambda b, i, c: (0, c))],
        out_specs=pl.BlockSpec((ts, cb), lambda b, i, c: (b * ns + i, c)),
        out_shape=jax.ShapeDtypeStruct((batch * seq, width), BF16),
        scratch_shapes=[pltpu.VMEM((ts + 2 * SUBLANES, cb), F32)],
        compiler_params=_params(("parallel", "parallel", "parallel")),
        name="dn_conv",
    )(p2, p2, p2, conv_w)


def _dn_gate_body(x_ref, alog_ref, dtb_ref, o_ref, *, heads):
    x = x_ref[...]
    z = x + dtb_ref[...]
    softplus = jnp.maximum(z, 0.0) + jnp.log1p(jnp.exp(-jnp.abs(z)))
    g = -jnp.exp(alog_ref[...]) * softplus
    lane = lax.broadcasted_iota(jnp.int32, x.shape, 1)
    zt = jnp.where(lane < LANES - 2 * heads, g, jax.nn.sigmoid(x)).T
    o_ref[0, 2 * heads:] = zt[LANES - 2 * heads:]
    pos = lax.broadcasted_iota(jnp.int32, (2 * heads, LANES), 1) % CHUNK
    fwd = lax.broadcasted_iota(jnp.int32, (2 * heads, LANES), 0) < heads
    for grp in range(zt.shape[1] // LANES):
        cols = slice(grp * LANES, (grp + 1) * LANES)
        gt = zt[LANES - 4 * heads:LANES - 2 * heads, cols]
        pre, suf = gt, gt
        s = 1
        while s < CHUNK:
            pre = pre + jnp.where(pos >= s, pltpu.roll(pre, s, 1), 0.0)
            suf = suf + jnp.where(pos < CHUNK - s, pltpu.roll(suf, LANES - s, 1), 0.0)
            s *= 2
        o_ref[0, :2 * heads, cols] = jnp.where(fwd, pre, suf)


def _dn_gates(p1, a_log, dt_bias, *, batch, seq, heads, col_block):
    ts = _tile(seq, 512)
    ns = seq // ts
    lo, hi = LANES - 4 * heads, LANES - 2 * heads
    alog = jnp.zeros((1, LANES), F32).at[0, lo:hi].set(a_log.reshape(-1))
    dtb = jnp.zeros((1, LANES), F32).at[0, lo:hi].set(dt_bias.reshape(-1))
    return pl.pallas_call(
        functools.partial(_dn_gate_body, heads=heads),
        grid=(batch, ns),
        in_specs=[pl.BlockSpec((ts, LANES), lambda b, i: (b * ns + i, col_block)),
                  pl.BlockSpec((1, LANES), lambda b, i: (0, 0)),
                  pl.BlockSpec((1, LANES), lambda b, i: (0, 0))],
        out_specs=pl.BlockSpec((1, 4 * heads, ts), lambda b, i: (b, 0, i)),
        out_shape=jax.ShapeDtypeStruct((batch, 4 * heads, seq), F32),
        compiler_params=_params(("parallel", "parallel")),
        name="dn_gates",
    )(p1, alog, dtb)


def _split_dot(a, b):
    ah, bh = a.astype(BF16), b.astype(BF16)
    al = (a - ah.astype(F32)).astype(BF16)
    bl = (b - bh.astype(F32)).astype(BF16)
    return _dot(ah, bh) + _dot(ah, bl) + _dot(al, bh)


def _unit_triangular_inverses(ms, eye):
    a = [eye + m for m in ms]
    ab = [x.astype(BF16) for x in a]
    t = [eye - m for m in ms]
    for _ in range(int(math.log2(CHUNK)) - 2):
        r = [2.0 * eye - _dot(x, y.astype(BF16)) for x, y in zip(ab, t)]
        t = [_dot(x.astype(BF16), y.astype(BF16)) for x, y in zip(t, r)]
    r = [eye - _split_dot(x, y) for x, y in zip(a, t)]
    return [x + _dot(x.astype(BF16), y.astype(BF16)) for x, y in zip(t, r)]


def _dn_local_body(q_ref, k_ref, v_ref, grow_ref, glast_ref,
                   u_ref, w_ref, qd_ref, kd_ref, qk_ref, *, chunks):
    ii = lax.broadcasted_iota(jnp.int32, (CHUNK, CHUNK), 0)
    jj = lax.broadcasted_iota(jnp.int32, (CHUNK, CHUNK), 1)
    eye = (ii == jj).astype(F32)
    rows = [slice(c * CHUNK, (c + 1) * CHUNK) for c in range(chunks)]
    q = [q_ref[r, :] for r in rows]
    k = [k_ref[r, :] for r in rows]
    v = [v_ref[r, :] for r in rows]
    qk_kk = [_dot_nt(jnp.concatenate([x, y], axis=0), y) for x, y in zip(q, k)]

    chains = [(c, d) for c in range(chunks) for d in range(2)]
    g_row, b_row, g_col, g_last, decay, ms = [], [], [], [], [], []
    for c, d in chains:
        incl = (jj <= ii) if d == 0 else (jj >= ii)
        strict = (jj < ii) if d == 0 else (jj > ii)
        r = jnp.broadcast_to(grow_ref[0, d, 0, c:c + 1, :], (LANES, LANES))
        rt = r.T
        gc, bc = rt[:CHUNK], rt[CHUNK:]
        gr = r[:CHUNK, :CHUNK]
        dec = jnp.exp(jnp.where(incl, gc[:, :CHUNK] - gr, -jnp.inf))
        ms.append(jnp.where(strict, qk_kk[c][CHUNK:] * bc[:, :CHUNK] * dec, 0.0))
        g_row.append(gr)
        b_row.append(r[:CHUNK, CHUNK:])
        g_col.append(gc)
        g_last.append(jnp.broadcast_to(glast_ref[0, d, 0, c:c + 1, :], (CHUNK, LANES)))
        decay.append(dec)

    t_beta = [t * b for t, b in zip(_unit_triangular_inverses(ms, eye), b_row)]
    us = [_dot(t.astype(BF16), v[c]) for t, (c, d) in zip(t_beta, chains)]
    ws = [_dot((t * jnp.exp(g)).astype(BF16), k[c]) for t, g, (c, d) in zip(t_beta, g_row, chains)]
    for i, (c, d) in enumerate(chains):
        u_ref[0, d, 0, rows[c], :] = us[i]
        w_ref[0, d, 0, rows[c], :] = ws[i].astype(BF16)
        qk_ref[0, d, 0, rows[c], :] = (qk_kk[c][:CHUNK] * decay[i]).astype(BF16)
        qd_ref[0, d, 0, rows[c], :] = (q[c].astype(F32) * jnp.exp(g_col[i])).astype(BF16)
        kd_ref[0, d, 0, rows[c], :] = (k[c].astype(F32) * jnp.exp(g_last[i] - g_col[i])).astype(BF16)


def _dn_local(qkv, grow, glast, *, batch, seq, heads):
    nc = seq // CHUNK
    cbk = _tile(nc, 16)
    L = cbk * CHUNK
    nb = nc // cbk
    tok = lambda off: pl.BlockSpec((L, DN_K), lambda b, h, t: (b * nb + t, off + h))
    gate = pl.BlockSpec((1, 2, 1, cbk, LANES), lambda b, h, t: (b, 0, h, t, 0))
    out = lambda width: pl.BlockSpec((1, 2, 1, L, width), lambda b, h, t: (b, 0, h, t, 0))
    shp = lambda width, dt: jax.ShapeDtypeStruct((batch, 2, heads, seq, width), dt)
    return pl.pallas_call(
        functools.partial(_dn_local_body, chunks=cbk),
        grid=(batch, heads, nb),
        in_specs=[tok(0), tok(heads), tok(2 * heads), gate, gate],
        out_specs=[out(DN_V), out(DN_K), out(DN_K), out(DN_K), out(CHUNK)],
        out_shape=[shp(DN_V, F32), shp(DN_K, BF16), shp(DN_K, BF16), shp(DN_K, BF16), shp(CHUNK, BF16)],
        compiler_params=_params(("parallel", "parallel", "parallel")),
        name="dn_local",
    )(qkv, qkv, qkv, grow, glast)


def _dn_scan_body(*refs, chunks, hb):
    ins, (of_ref, ob_ref, st_ref) = refs[:12], refs[12:]
    t = pl.program_id(2)

    @pl.when(t == 0)
    def _():
        st_ref[...] = jnp.zeros_like(st_ref)

    chains = [(d, j) for d in range(2) for j in range(hb)]
    states = [st_ref[d, j] for d, j in chains]
    for ci in range(chunks):
        blk = []
        for d, j in chains:
            u_ref, w_ref, qd_ref, kd_ref, qk_ref, gl_ref = ins[6 * d:6 * d + 6]
            c = ci if d == 0 else chunks - 1 - ci
            rows = slice(c * CHUNK, (c + 1) * CHUNK)
            blk.append((u_ref[0, 0, j, rows, :], w_ref[0, 0, j, rows, :], qd_ref[0, 0, j, rows, :],
                        kd_ref[0, 0, j, rows, :], qk_ref[0, 0, j, rows, :],
                        jnp.exp(gl_ref[0, 0, j, c:c + 1, :]), rows))
        sb = [s.astype(BF16) for s in states]
        w_s = [_dot(b[1], s) for b, s in zip(blk, sb)]
        q_s = [_dot(b[2], s) for b, s in zip(blk, sb)]
        vb = [(b[0] - x).astype(BF16) for b, x in zip(blk, w_s)]
        outs = [x + _dot(b[4], y) for b, x, y in zip(blk, q_s, vb)]
        states = [s * b[5] + _dot_tn(b[3], y) for b, s, y in zip(blk, states, vb)]
        for (d, j), b, o in zip(chains, blk, outs):
            o_ref = of_ref if d == 0 else ob_ref
            o_ref[b[6], j * DN_V:(j + 1) * DN_V] = o
    for (d, j), s in zip(chains, states):
        st_ref[d, j] = s


def _dn_scan(u, w, qd, kd, qk, glast, *, batch, seq, heads):
    nc = seq // CHUNK
    cbk = _tile(nc, 8)
    L = cbk * CHUNK
    nb = nc // cbk
    hb = _tile(heads, 8)

    def spec(d, rows, width):
        if d == 0:
            return pl.BlockSpec((1, 1, hb, rows, width), lambda b, g, t: (b, 0, g, t, 0))
        return pl.BlockSpec((1, 1, hb, rows, width), lambda b, g, t: (b, 1, g, nb - 1 - t, 0))

    in_specs, args = [], []
    for d in range(2):
        in_specs += [spec(d, L, DN_V), spec(d, L, DN_K), spec(d, L, DN_K), spec(d, L, DN_K),
                     spec(d, L, CHUNK), spec(d, cbk, LANES)]
        args += [u, w, qd, kd, qk, glast]
    out_shape = jax.ShapeDtypeStruct((batch * seq, heads * DN_V), F32)
    return pl.pallas_call(
        functools.partial(_dn_scan_body, chunks=cbk, hb=hb),
        grid=(batch, heads // hb, nb),
        in_specs=in_specs,
        out_specs=[pl.BlockSpec((L, hb * DN_V), lambda b, g, t: (b * nb + t, g)),
                   pl.BlockSpec((L, hb * DN_V), lambda b, g, t: (b * nb + nb - 1 - t, g))],
        out_shape=[out_shape, out_shape],
        scratch_shapes=[pltpu.VMEM((2, hb, DN_K, DN_V), F32)],
        compiler_params=_params(("parallel", "parallel", "arbitrary")),
        name="dn_scan",
    )(*args)


def _dn_out_body(of_ref, ob_ref, z_ref, g_ref, o_ref):
    o = of_ref[...] + ob_ref[...]
    z = z_ref[...]
    g = g_ref[...]
    for h in range(o.shape[1] // DN_V):
        sl = slice(h * DN_V, (h + 1) * DN_V)
        oh, zh = o[:, sl], z[:, sl]
        y = oh * lax.rsqrt(jnp.mean(oh * oh, axis=-1, keepdims=True) + EPS) * g
        o_ref[:, sl] = (y * (zh * jax.nn.sigmoid(zh))).astype(o_ref.dtype)


def _dn_out(o_f, o_b, p2, norm_g, *, z_col0):
    M, W = o_f.shape
    tm, tw = _tile(M, 512), _tile(W, 512)
    zj = z_col0 // tw
    return pl.pallas_call(
        _dn_out_body,
        grid=(M // tm, W // tw),
        in_specs=[pl.BlockSpec((tm, tw), lambda i, j: (i, j)),
                  pl.BlockSpec((tm, tw), lambda i, j: (i, j)),
                  pl.BlockSpec((tm, tw), lambda i, j: (i, zj + j)),
                  pl.BlockSpec((1, DN_V), lambda i, j: (0, 0))],
        out_specs=pl.BlockSpec((tm, tw), lambda i, j: (i, j)),
        out_shape=jax.ShapeDtypeStruct((M, W), BF16),
        compiler_params=_params(("parallel", "parallel")),
        name="dn_out",
    )(o_f, o_b, p2, norm_g.reshape(1, DN_V))


def _rope_table(positions):
    half = ROPE // 2
    inv_freq = ROPE_THETA ** (-jnp.arange(half, dtype=F32) / half)
    ang = positions.astype(F32)[..., None] * inv_freq
    cos, sin = jnp.cos(ang), jnp.sin(ang)
    return jnp.concatenate([cos, cos, sin, sin], axis=-1).reshape(-1, 4 * half)


def _mixer(u, tab, w_in, q_norm_g, kv_norm_g, w_uq, w_ukv, w_branch_a, conv_w, a_log, dt_bias,
           o_norm_g, w_branch_b, w_out, *, batch, seq, stream, gate, alpha):
    d_model = u.shape[1]
    q_rank, kv_rank = q_norm_g.shape[0], kv_norm_g.shape[0]
    mla_heads = w_uq.shape[1] // MLA_QK
    dn_heads = a_log.shape[1]
    dn_qk_w, dn_v_w = dn_heads * DN_K, dn_heads * DN_V
    c_qkv = q_rank + kv_rank + ROPE
    c_z = c_qkv + 2 * dn_qk_w + dn_v_w
    c_a = c_z + dn_v_w
    c_gate = c_a + 4 * dn_heads
    assert ROPE + 4 * dn_heads == LANES and c_qkv - ROPE == (q_rank + kv_rank)
    w1 = jnp.concatenate([w_in[:, :c_qkv], w_in[:, c_a:c_gate]], axis=1).astype(BF16)
    w2 = jnp.concatenate([w_in[:, c_qkv:c_a], w_in[:, c_gate:]], axis=1).astype(BF16)
    p1 = _matmul(u, w1, F32, tm=512, tn=w1.shape[1])
    p2 = _matmul(u, w2, F32)

    wq = w_uq.reshape(q_rank, mla_heads, MLA_QK)
    r1, r2 = wq[..., NOPE:NOPE + ROPE // 2], wq[..., NOPE + ROPE // 2:]
    wq = jnp.concatenate([wq, -r2, r1], axis=-1).transpose(1, 0, 2).astype(BF16)
    wkv = w_ukv.reshape(kv_rank, mla_heads, NOPE + MLA_V).transpose(1, 0, 2).astype(BF16)
    q, k, v = _mla_prep(p1, tab, q_norm_g, kv_norm_g, wq, wkv, batch=batch, seq=seq)
    o_a = _flash(q, k, v).reshape(batch * seq, mla_heads * MLA_V)

    qkv = _dn_conv(p2, conv_w, batch=batch, seq=seq, heads=dn_heads)
    gates = _dn_gates(p1, a_log, dt_bias, batch=batch, seq=seq, heads=dn_heads,
                      col_block=(q_rank + kv_rank) // LANES)
    nc = seq // CHUNK
    gates = gates.reshape(batch, 2, 2, dn_heads, nc, CHUNK)
    grow = gates.transpose(0, 2, 3, 4, 1, 5).reshape(batch, 2, dn_heads, nc, 2 * CHUNK)
    g_cum = gates[:, 0]
    glast = jnp.stack([g_cum[:, 0, :, :, CHUNK - 1], g_cum[:, 1, :, :, 0]], axis=1)
    glast = jnp.broadcast_to(glast[..., None], glast.shape + (LANES,))
    u_, w_, qd, kd, qk = _dn_local(qkv, grow, glast, batch=batch, seq=seq, heads=dn_heads)
    o_f, o_b = _dn_scan(u_, w_, qd, kd, qk, glast, batch=batch, seq=seq, heads=dn_heads)
    o_bn = _dn_out(o_f, o_b, p2, o_norm_g, z_col0=2 * dn_qk_w + dn_v_w)

    merged = _merge(o_a, o_bn, w_branch_a.astype(BF16), w_branch_b.astype(BF16), p2, d_model,
                    gate_col0=2 * dn_qk_w + 2 * dn_v_w)
    return _matmul(merged, w_out.astype(BF16), F32, stream=stream, gate=gate, seq=seq,
                   residual=(alpha, 1.0), columns_outer=True)


def _ffn(h, w_gate, w_up, w_down, layer, *, stream, gate, seq, alpha):
    a = _gateup(h, w_gate, w_up, layer)
    return _matmul(a, w_down[layer].astype(BF16), F32, tm=512, tn=1024, tk=w_down.shape[1],
                   stream=stream, gate=gate, seq=seq, residual=(alpha, 0.5), columns_outer=True)


def kernel(x, c, positions, ln_in_g, ln_in_b, w_ada, b_ada, ada_table, ffn1_w_gate, ffn1_w_up, ffn1_w_down, w_in, mla_q_norm_g, mla_kv_norm_g, mla_w_uq, mla_w_ukv, w_branch_a, dn_conv_w, dn_a_log, dn_dt_bias, dn_norm_g, w_branch_b, w_out, ffn2_w_gate, ffn2_w_up, ffn2_w_down, post_ln_g, post_ln_b):
    B, S, D = x.shape
    depth = ada_table.shape[0]
    alpha = (2.0 * depth) ** 0.25
    cond = _adaln(c, w_ada, b_ada).reshape(B, N_SUB, 3, D)
    mod = cond[None] + ada_table[:, None]
    tab = _rope_table(positions)
    post = functools.partial(_post, seq=S)

    xs, g, b = x.reshape(B * S, D), ln_in_g, ln_in_b
    h, stats = post(xs, g, b, mod[0, :, 0, 0], mod[0, :, 0, 1])
    for l in range(depth):
        xs = _ffn(h, ffn1_w_gate, ffn1_w_up, ffn1_w_down, l, stream=(xs, stats, g, b), gate=mod[l, :, 0, 2],
                  seq=S, alpha=alpha)
        g, b = post_ln_g[l, 0], post_ln_b[l, 0]
        h, stats = post(xs, g, b, mod[l, :, 1, 0], mod[l, :, 1, 1])
        xs = _mixer(h, tab, w_in[l], mla_q_norm_g[l], mla_kv_norm_g[l], mla_w_uq[l], mla_w_ukv[l],
                    w_branch_a[l], dn_conv_w[l], dn_a_log[l], dn_dt_bias[l], dn_norm_g[l],
                    w_branch_b[l], w_out[l], batch=B, seq=S, stream=(xs, stats, g, b), gate=mod[l, :, 1, 2],
                    alpha=alpha)
        g, b = post_ln_g[l, 1], post_ln_b[l, 1]
        h, stats = post(xs, g, b, mod[l, :, 2, 0], mod[l, :, 2, 1])
        xs = _ffn(h, ffn2_w_gate, ffn2_w_up, ffn2_w_down, l, stream=(xs, stats, g, b), gate=mod[l, :, 2, 2],
                  seq=S, alpha=alpha)
        g, b = post_ln_g[l, 2], post_ln_b[l, 2]
        if l + 1 < depth:
            h, stats = post(xs, g, b, mod[l + 1, :, 0, 0], mod[l + 1, :, 0, 1])
    return post(xs, g, b, None, None).reshape(B, S, D)
```

```python
import functools
import math

import jax
import jax.numpy as jnp
from jax import lax
from jax.experimental import pallas as pl
from jax.experimental.pallas import tpu as pltpu

F32 = jnp.float32
BF16 = jnp.bfloat16

NOPE = 128
ROPE = 64
MLA_V = 128
MLA_QK = NOPE + ROPE
DN_K = 128
DN_V = 128
DN_CONV = 5
CHUNK = 64
N_SUB = 3
EPS = 1e-6
ROPE_THETA = 10000.0
LOG2E = math.log2(math.e)

LANES = 128
SUBLANES = 8
V7X_VMEM_LIMIT_BYTES = 56 * 1024 * 1024


def _tile(dim, pref):
    t = min(dim, pref)
    assert dim % t == 0, (dim, pref)
    return t


def _params(sem):
    return pltpu.CompilerParams(dimension_semantics=sem, vmem_limit_bytes=V7X_VMEM_LIMIT_BYTES)


def _dot(a, b):
    return jnp.dot(a, b, preferred_element_type=F32)


def _dot_nt(a, b):
    return lax.dot_general(a, b, (((1,), (1,)), ((), ())), preferred_element_type=F32)


def _dot_tn(a, b):
    return lax.dot_general(a, b, (((0,), (0,)), ((), ())), preferred_element_type=F32)


def _adaln_body(c_ref, w_ref, b_ref, o_ref):
    c = c_ref[...]
    h = (c * jax.nn.sigmoid(c)).astype(BF16)
    o_ref[...] = _dot(h, w_ref[...].astype(BF16)) + b_ref[...]


def _adaln(c, w_ada, b_ada):
    B, D = c.shape
    N = w_ada.shape[1]
    rows = max(SUBLANES, B)
    c_pad = jnp.zeros((rows, D), F32).at[:B].set(c)
    tn = _tile(N, 512)
    out = pl.pallas_call(
        _adaln_body,
        grid=(N // tn,),
        in_specs=[pl.BlockSpec((rows, D), lambda j: (0, 0)),
                  pl.BlockSpec((D, tn), lambda j: (0, j)),
                  pl.BlockSpec((1, tn), lambda j: (0, j))],
        out_specs=pl.BlockSpec((rows, tn), lambda j: (0, j)),
        out_shape=jax.ShapeDtypeStruct((rows, N), F32),
        compiler_params=_params(("parallel",)),
        name="adaln",
    )(c_pad, w_ada, b_ada.reshape(1, N))
    return out[:B]


def _post_body(*refs, has_mod):
    it = iter(refs)
    x_ref, g_ref, b_ref = next(it), next(it), next(it)
    if has_mod:
        sh_ref, sc_ref, h_ref, st_ref = next(it), next(it), next(it), next(it)
    else:
        xo_ref = next(it)
    x = x_ref[...]
    mu = jnp.mean(x, axis=-1, keepdims=True)
    xc = x - mu
    var = jnp.mean(xc * xc, axis=-1, keepdims=True)
    rstd = lax.rsqrt(var + EPS)
    y = xc * rstd * g_ref[...] + b_ref[...]
    if has_mod:
        h_ref[...] = (y * (1.0 + sc_ref[0]) + sh_ref[0]).astype(BF16)
        st_ref[:, :LANES] = jnp.broadcast_to(mu, (mu.shape[0], LANES))
        st_ref[:, LANES:] = jnp.broadcast_to(rstd, (rstd.shape[0], LANES))
    else:
        xo_ref[...] = y


def _post(x, g, b, shift, scale, *, seq):
    M, D = x.shape
    has_mod = shift is not None
    tm = _tile(seq, 256)
    per_b = seq // tm
    row = pl.BlockSpec((tm, D), lambda i: (i, 0))
    vec = pl.BlockSpec((1, D), lambda i: (0, 0))
    bvec = pl.BlockSpec((1, 1, D), lambda i: (i // per_b, 0, 0))
    args = [x, g.reshape(1, D), b.reshape(1, D)]
    specs = [row, vec, vec]
    if has_mod:
        args += [shift.reshape(-1, 1, D), scale.reshape(-1, 1, D)]
        specs += [bvec, bvec]
        out_shape = [jax.ShapeDtypeStruct((M, D), BF16), jax.ShapeDtypeStruct((M, 2 * LANES), F32)]
        out_specs = [row, pl.BlockSpec((tm, 2 * LANES), lambda i: (i, 0))]
    else:
        out_shape = [jax.ShapeDtypeStruct((M, D), F32)]
        out_specs = [row]
    outs = pl.pallas_call(
        functools.partial(_post_body, has_mod=has_mod),
        grid=(M // tm,),
        in_specs=specs,
        out_specs=out_specs,
        out_shape=out_shape,
        compiler_params=_params(("parallel",)),
        name="post_ln",
    )(*args)
    return outs if has_mod else outs[0]


def _mm_body(*refs, nk, residual):
    a_ref, w_ref = refs[:2]
    refs = refs[2:]
    if residual is not None:
        (x_ref, st_ref, lg_ref, lb_ref, gate_ref), refs = refs[:5], refs[5:]
    o_ref = refs[0]

    def finish(acc):
        if residual is None:
            o_ref[...] = acc.astype(o_ref.dtype)
            return
        alpha, r = residual
        mu, rstd = st_ref[:, :LANES], st_ref[:, LANES:]
        for s in range(acc.shape[1] // LANES):
            sl = slice(s * LANES, (s + 1) * LANES)
            xn = (x_ref[:, sl] - mu) * rstd * lg_ref[:, sl] + lb_ref[:, sl]
            o_ref[:, sl] = alpha * xn + (r * gate_ref[0][:, sl]) * acc[:, sl]

    if nk == 1:
        finish(_dot(a_ref[...], w_ref[...]))
        return
    acc_ref = refs[1]
    k = pl.program_id(2)

    @pl.when(k == 0)
    def _():
        acc_ref[...] = jnp.zeros_like(acc_ref)

    acc_ref[...] += _dot(a_ref[...], w_ref[...])

    @pl.when(k == nk - 1)
    def _():
        finish(acc_ref[...])


def _matmul(a, w, out_dtype, *, tm=1024, tn=1024, tk=4096, stream=None, gate=None, seq=None, residual=None,
            columns_outer=False):
    M, K = a.shape
    N = w.shape[1]
    tm, tn, tk = _tile(M if residual is None else seq, tm), _tile(N, tn), _tile(K, tk)
    nk = K // tk
    if columns_outer:
        assert nk == 1
        grid = (N // tn, M // tm, nk)
        at = lambda f: (lambda j, i, k: f(i, j, k))
        w_spec = pl.BlockSpec((tk, tn), at(lambda i, j, k: (k, j)), pipeline_mode=pl.Buffered(1))
    else:
        grid = (M // tm, N // tn, nk)
        at = lambda f: f
        w_spec = pl.BlockSpec((tk, tn), lambda i, j, k: (k, j))
    args = [a, w]
    in_specs = [pl.BlockSpec((tm, tk), at(lambda i, j, k: (i, k))), w_spec]
    if residual is not None:
        per_b = seq // tm
        x, stats, g, b = stream
        args += [x, stats, g.reshape(1, N), b.reshape(1, N), gate.reshape(-1, 1, N)]
        in_specs += [pl.BlockSpec((tm, tn), at(lambda i, j, k: (i, j))),
                     pl.BlockSpec((tm, 2 * LANES), at(lambda i, j, k: (i, 0))),
                     pl.BlockSpec((1, tn), at(lambda i, j, k: (0, j))),
                     pl.BlockSpec((1, tn), at(lambda i, j, k: (0, j))),
                     pl.BlockSpec((1, 1, tn), at(lambda i, j, k: (i // per_b, 0, j)))]
    return pl.pallas_call(
        functools.partial(_mm_body, nk=nk, residual=residual),
        grid=grid,
        in_specs=in_specs,
        out_specs=pl.BlockSpec((tm, tn), at(lambda i, j, k: (i, j))),
        out_shape=jax.ShapeDtypeStruct((M, N), out_dtype),
        scratch_shapes=[pltpu.VMEM((tm, tn), F32)] if nk > 1 else [],
        compiler_params=_params(("parallel", "parallel", "arbitrary")),
        name="matmul",
    )(*args)


def _gateup_body(a_ref, wg_ref, wu_ref, o_ref, wgb_ref, wub_ref):
    @pl.when(pl.program_id(1) == 0)
    def _():
        wgb_ref[...] = wg_ref[0].astype(BF16)
        wub_ref[...] = wu_ref[0].astype(BF16)

    a = a_ref[...]
    g = _dot(a, wgb_ref[...])
    u = _dot(a, wub_ref[...])
    o_ref[...] = (g * jax.nn.sigmoid(g) * u).astype(o_ref.dtype)


def _gateup(a, wg, wu, layer):
    M, K = a.shape
    N = wg.shape[2]
    tm, tn = _tile(M, 1024), _tile(N, 512)
    weight = pl.BlockSpec((1, K, tn), lambda j, i: (layer, 0, j), pipeline_mode=pl.Buffered(1))
    return pl.pallas_call(
        _gateup_body,
        grid=(N // tn, M // tm),
        in_specs=[pl.BlockSpec((tm, K), lambda j, i: (i, 0)), weight, weight],
        out_specs=pl.BlockSpec((tm, tn), lambda j, i: (i, j)),
        out_shape=jax.ShapeDtypeStruct((M, N), BF16),
        scratch_shapes=[pltpu.VMEM((K, tn), BF16), pltpu.VMEM((K, tn), BF16)],
        compiler_params=_params(("parallel", "arbitrary")),
        name="ffn_gate_up",
    )(a, wg, wu)


def _merge_body(oa_ref, ob_ref, wa_ref, wb_ref, ga_ref, gb_ref, o_ref):
    ya = _dot(oa_ref[...], wa_ref[...])
    yb = _dot(ob_ref[...], wb_ref[...])
    o_ref[...] = (jax.nn.sigmoid(ga_ref[...]) * ya + jax.nn.sigmoid(gb_ref[...]) * yb).astype(o_ref.dtype)


def _merge(oa, ob, wa, wb, p2, d_model, gate_col0):
    M, Ka = oa.shape
    Kb = ob.shape[1]
    tm, tn = _tile(M, 1024), _tile(d_model, 1024)
    ja = gate_col0 // tn
    jb = (gate_col0 + d_model) // tn
    return pl.pallas_call(
        _merge_body,
        grid=(d_model // tn, M // tm),
        in_specs=[pl.BlockSpec((tm, Ka), lambda j, i: (i, 0)),
                  pl.BlockSpec((tm, Kb), lambda j, i: (i, 0)),
                  pl.BlockSpec((Ka, tn), lambda j, i: (0, j), pipeline_mode=pl.Buffered(1)),
                  pl.BlockSpec((Kb, tn), lambda j, i: (0, j), pipeline_mode=pl.Buffered(1)),
                  pl.BlockSpec((tm, tn), lambda j, i: (i, ja + j)),
                  pl.BlockSpec((tm, tn), lambda j, i: (i, jb + j))],
        out_specs=pl.BlockSpec((tm, tn), lambda j, i: (i, j)),
        out_shape=jax.ShapeDtypeStruct((M, d_model), BF16),
        compiler_params=_params(("parallel", "parallel")),
        name="branch_merge",
    )(oa, ob, wa, wb, p2, p2)


def _rms(x, g):
    return x * lax.rsqrt(jnp.mean(x * x, axis=-1, keepdims=True) + EPS) * g


def _mla_prep_body(ql_ref, kvl_ref, kr_ref, tab_ref, gq_ref, gkv_ref, wq_ref, wkv_ref,
                   q_ref, k_ref, v_ref, *, heads, q_scale):
    qn = _rms(ql_ref[...], gq_ref[...]).astype(BF16)
    kvn = _rms(kvl_ref[...], gkv_ref[...]).astype(BF16)
    tab = tab_ref[...]
    tab_sin = pltpu.roll(tab, 64, 1)
    x = kr_ref[...]
    lane = lax.broadcasted_iota(jnp.int32, x.shape, 1)
    x_rot = jnp.where(lane < ROPE // 2, -pltpu.roll(x, LANES - ROPE // 2, 1), pltpu.roll(x, ROPE // 2, 1))
    k_rope_t = (x * tab + x_rot * tab_sin).T[:ROPE].astype(BF16)
    for h in range(heads):
        r = _dot(qn, wq_ref[h])
        p = r[:, NOPE:] * tab
        roped = p + pltpu.roll(p, 64, 1)
        q_ref[0, h, :, :NOPE] = (r[:, :NOPE] * q_scale).astype(BF16)
        q_ref[0, h, :, NOPE:] = (roped[:, :ROPE] * q_scale).astype(BF16)
        kv = _dot(kvn, wkv_ref[h])
        k_ref[0, h, :NOPE, :] = kv[:, :NOPE].T.astype(BF16)
        k_ref[0, h, NOPE:, :] = k_rope_t
        v_ref[0, h] = kv[:, NOPE:].astype(BF16)


def _mla_prep(p1, tab, gq, gkv, wq, wkv, *, batch, seq):
    heads, q_rank, _ = wq.shape
    kv_rank = wkv.shape[1]
    ts = _tile(seq, 512)
    ns = seq // ts
    row = lambda b, i: b * ns + i
    return pl.pallas_call(
        functools.partial(_mla_prep_body, heads=heads, q_scale=MLA_QK ** -0.5 * LOG2E),
        grid=(batch, ns),
        in_specs=[pl.BlockSpec((ts, q_rank), lambda b, i: (row(b, i), 0)),
                  pl.BlockSpec((ts, kv_rank), lambda b, i: (row(b, i), q_rank // kv_rank)),
                  pl.BlockSpec((ts, LANES), lambda b, i: (row(b, i), (q_rank + kv_rank) // LANES)),
                  pl.BlockSpec((ts, LANES), lambda b, i: (row(b, i), 0)),
                  pl.BlockSpec((1, q_rank), lambda b, i: (0, 0)),
                  pl.BlockSpec((1, kv_rank), lambda b, i: (0, 0)),
                  pl.BlockSpec((heads, q_rank, 2 * LANES), lambda b, i: (0, 0, 0)),
                  pl.BlockSpec((heads, kv_rank, 2 * LANES), lambda b, i: (0, 0, 0))],
        out_specs=[pl.BlockSpec((1, heads, ts, MLA_QK), lambda b, i: (b, 0, i, 0)),
                   pl.BlockSpec((1, heads, MLA_QK, ts), lambda b, i: (b, 0, 0, i)),
                   pl.BlockSpec((1, heads, ts, MLA_V), lambda b, i: (b, 0, i, 0))],
        out_shape=[jax.ShapeDtypeStruct((batch, heads, seq, MLA_QK), BF16),
                   jax.ShapeDtypeStruct((batch, heads, MLA_QK, seq), BF16),
                   jax.ShapeDtypeStruct((batch, heads, seq, MLA_V), BF16)],
        compiler_params=_params(("parallel", "parallel")),
        name="mla_prep",
    )(p1, p1, p1, tab, gq.reshape(1, -1), gkv.reshape(1, -1), wq, wkv)


def _flash_body(q_ref, kt_ref, v_ref, o_ref, s_ref, m_ref, l_ref, acc_ref, *, tk, nk):
    m_ref[...] = jnp.full_like(m_ref, -jnp.inf)
    l_ref[...] = jnp.zeros_like(l_ref)
    acc_ref[...] = jnp.zeros_like(acc_ref)
    q = q_ref[0, 0]

    def key_rows(j):
        return slice(j * tk, (j + 1) * tk)

    def scores(j):
        return _dot(q, kt_ref[0, 0, :, key_rows(j)])

    s_ref[0] = scores(0)
    for j in range(nk):
        s = s_ref[j % 2]
        if j + 1 < nk:
            s_ref[(j + 1) % 2] = scores(j + 1)
        m_prev = m_ref[...]
        m_new = jnp.maximum(m_prev, jnp.max(s, axis=1, keepdims=True))
        alpha = jnp.exp2(m_prev - m_new)
        p = jnp.exp2(s - m_new[:, :1])
        l_ref[...] = alpha * l_ref[...] + jnp.sum(p, axis=1, keepdims=True)
        acc_ref[...] = alpha * acc_ref[...] + _dot(p.astype(BF16), v_ref[0, 0, key_rows(j), :])
        m_ref[...] = m_new
    o_ref[0] = (acc_ref[...] / l_ref[...]).astype(o_ref.dtype)


def _flash(q, k, v):
    B, H, S, _ = q.shape
    tq, tk = _tile(S, 1024), _tile(S, 2048)
    return pl.pallas_call(
        functools.partial(_flash_body, tk=tk, nk=S // tk),
        grid=(B, H, S // tq),
        in_specs=[pl.BlockSpec((1, 1, tq, MLA_QK), lambda b, h, i: (b, h, i, 0)),
                  pl.BlockSpec((1, 1, MLA_QK, S), lambda b, h, i: (b, h, 0, 0)),
                  pl.BlockSpec((1, 1, S, MLA_V), lambda b, h, i: (b, h, 0, 0))],
        out_specs=pl.BlockSpec((1, tq, MLA_V), lambda b, h, i: (b, i, h)),
        out_shape=jax.ShapeDtypeStruct((B, S, H * MLA_V), BF16),
        scratch_shapes=[pltpu.VMEM((2, tq, tk), F32), pltpu.VMEM((tq, LANES), F32),
                        pltpu.VMEM((tq, LANES), F32), pltpu.VMEM((tq, MLA_V), F32)],
        compiler_params=_params(("parallel", "parallel", "arbitrary")),
        name="mla_flash",
    )(q, k, v)


def _dn_conv_body(cur_ref, prev_ref, next_ref, w_ref, o_ref, buf_ref, *, ts, ns, qk_blocks):
    i = pl.program_id(1)
    c = pl.program_id(2)
    pad = DN_CONV // 2
    buf_ref[0:SUBLANES] = jnp.where(i == 0, 0.0, prev_ref[...])
    buf_ref[SUBLANES:SUBLANES + ts] = cur_ref[...]
    buf_ref[SUBLANES + ts:] = jnp.where(i == ns - 1, 0.0, next_ref[...])
    w = w_ref[...]
    full = buf_ref[...]
    rows = full.shape[0]
    acc = w[pad:pad + 1] * full[SUBLANES:SUBLANES + ts]
    for t in range(DN_CONV):
        if t != pad:
            acc = acc + w[t:t + 1] * pltpu.roll(full, (pad - t) % rows, 0)[SUBLANES:SUBLANES + ts]
    y = acc * jax.nn.sigmoid(acc)
    kind = c // qk_blocks
    for g in range(y.shape[1] // DN_K):
        yg = y[:, g * DN_K:(g + 1) * DN_K]
        inv = lax.rsqrt(jnp.sum(yg * yg, axis=-1, keepdims=True) + EPS)
        inv = jnp.where(kind < 2, inv, 1.0) * jnp.where(kind == 0, DN_K ** -0.5, 1.0)
        o_ref[:, g * DN_K:(g + 1) * DN_K] = (yg * inv).astype(o_ref.dtype)


def _dn_conv(p2, conv_w, *, batch, seq, heads):
    width = 3 * heads * DN_K
    ts = _tile(seq, 512)
    ns = seq // ts
    cb = 4 * DN_K
    qk_blocks = heads * DN_K // cb
    r8 = ts // SUBLANES
    last8 = batch * seq // SUBLANES - 1
    return pl.pallas_call(
        functools.partial(_dn_conv_body, ts=ts, ns=ns, qk_blocks=qk_blocks),
        grid=(batch, ns, width // cb),
        in_specs=[pl.BlockSpec((ts, cb), lambda b, i, c: (b * ns + i, c)),
                  pl.BlockSpec((SUBLANES, cb), lambda b, i, c: (jnp.maximum((b * ns + i) * r8 - 1, 0), c)),
                  pl.BlockSpec((SUBLANES, cb), lambda b, i, c: (jnp.minimum((b * ns + i + 1) * r8, last8), c)),
                  pl.BlockSpec((DN_CONV, cb), lambda b, i, c: (0, c))],
        out_specs=pl.BlockSpec((ts, cb), lambda b, i, c: (b * ns + i, c)),
        out_shape=jax.ShapeDtypeStruct((batch * seq, width), BF16),
        scratch_shapes=[pltpu.VMEM((ts + 2 * SUBLANES, cb), F32)],
        compiler_params=_params(("parallel", "parallel", "parallel")),
        name="dn_conv",
    )(p2, p2, p2, conv_w)


def _dn_gate_body(x_ref, alog_ref, dtb_ref, o_ref, *, heads):
    x = x_ref[...]
    z = x + dtb_ref[...]
    softplus = jnp.maximum(z, 0.0) + jnp.log1p(jnp.exp(-jnp.abs(z)))
    g = -jnp.exp(alog_ref[...]) * softplus
    lane = lax.broadcasted_iota(jnp.int32, x.shape, 1)
    zt = jnp.where(lane < LANES - 2 * heads, g, jax.nn.sigmoid(x)).T
    o_ref[0, 2 * heads:] = zt[LANES - 2 * heads:]
    pos = lax.broadcasted_iota(jnp.int32, (2 * heads, LANES), 1) % CHUNK
    fwd = lax.broadcasted_iota(jnp.int32, (2 * heads, LANES), 0) < heads
    for grp in range(zt.shape[1] // LANES):
        cols = slice(grp * LANES, (grp + 1) * LANES)
        gt = zt[LANES - 4 * heads:LANES - 2 * heads, cols]
        pre, suf = gt, gt
        s = 1
        while s < CHUNK:
            pre = pre + jnp.where(pos >= s, pltpu.roll(pre, s, 1), 0.0)
            suf = suf + jnp.where(pos < CHUNK - s, pltpu.roll(suf, LANES - s, 1), 0.0)
            s *= 2
        o_ref[0, :2 * heads, cols] = jnp.where(fwd, pre, suf)


def _dn_gates(p1, a_log, dt_bias, *, batch, seq, heads, col_block):
    ts = _tile(seq, 512)
    ns = seq // ts
    lo, hi = LANES - 4 * heads, LANES - 2 * heads
    alog = jnp.zeros((1, LANES), F32).at[0, lo:hi].set(a_log.reshape(-1))
    dtb = jnp.zeros((1, LANES), F32).at[0, lo:hi].set(dt_bias.reshape(-1))
    return pl.pallas_call(
        functools.partial(_dn_gate_body, heads=heads),
        grid=(batch, ns),
        in_specs=[pl.BlockSpec((ts, LANES), lambda b, i: (b * ns + i, col_block)),
                  pl.BlockSpec((1, LANES), lambda b, i: (0, 0)),
                  pl.BlockSpec((1, LANES), lambda b, i: (0, 0))],
        out_specs=pl.BlockSpec((1, 4 * heads, ts), lambda b, i: (b, 0, i)),
        out_shape=jax.ShapeDtypeStruct((batch, 4 * heads, seq), F32),
        compiler_params=_params(("parallel", "parallel")),
        name="dn_gates",
    )(p1, alog, dtb)


def _split_dot(a, b):
    ah, bh = a.astype(BF16), b.astype(BF16)
    al = (a - ah.astype(F32)).astype(BF16)
    bl = (b - bh.astype(F32)).astype(BF16)
    return _dot(ah, bh) + _dot(ah, bl) + _dot(al, bh)


def _unit_triangular_inverses(ms, eye):
    a = [eye + m for m in ms]
    ab = [x.astype(BF16) for x in a]
    t = [eye - m for m in ms]
    for _ in range(int(math.log2(CHUNK)) - 2):
        r = [2.0 * eye - _dot(x, y.astype(BF16)) for x, y in zip(ab, t)]
        t = [_dot(x.astype(BF16), y.astype(BF16)) for x, y in zip(t, r)]
    r = [eye - _split_dot(x, y) for x, y in zip(a, t)]
    return [x + _dot(x.astype(BF16), y.astype(BF16)) for x, y in zip(t, r)]


def _dn_local_body(q_ref, k_ref, v_ref, grow_ref, glast_ref,
                   u_ref, w_ref, qd_ref, kd_ref, qk_ref, *, chunks):
    ii = lax.broadcasted_iota(jnp.int32, (CHUNK, CHUNK), 0)
    jj = lax.broadcasted_iota(jnp.int32, (CHUNK, CHUNK), 1)
    eye = (ii == jj).astype(F32)
    rows = [slice(c * CHUNK, (c + 1) * CHUNK) for c in range(chunks)]
    q = [q_ref[r, :] for r in rows]
    k = [k_ref[r, :] for r in rows]
    v = [v_ref[r, :] for r in rows]
    qk_kk = [_dot_nt(jnp.concatenate([x, y], axis=0), y) for x, y in zip(q, k)]

    chains = [(c, d) for c in range(chunks) for d in range(2)]
    g_row, b_row, g_col, g_last, decay, ms = [], [], [], [], [], []
    for c, d in chains:
        incl = (jj <= ii) if d == 0 else (jj >= ii)
        strict = (jj < ii) if d == 0 else (jj > ii)
        r = jnp.broadcast_to(grow_ref[0, d, 0, c:c + 1, :], (LANES, LANES))
        rt = r.T
        gc, bc = rt[:CHUNK], rt[CHUNK:]
        gr = r[:CHUNK, :CHUNK]
        dec = jnp.exp(jnp.where(incl, gc[:, :CHUNK] - gr, -jnp.inf))
        ms.append(jnp.where(strict, qk_kk[c][CHUNK:] * bc[:, :CHUNK] * dec, 0.0))
        g_row.append(gr)
        b_row.append(r[:CHUNK, CHUNK:])
        g_col.append(gc)
        g_last.append(jnp.broadcast_to(glast_ref[0, d, 0, c:c + 1, :], (CHUNK, LANES)))
        decay.append(dec)

    t_beta = [t * b for t, b in zip(_unit_triangular_inverses(ms, eye), b_row)]
    us = [_dot(t.astype(BF16), v[c]) for t, (c, d) in zip(t_beta, chains)]
    ws = [_dot((t * jnp.exp(g)).astype(BF16), k[c]) for t, g, (c, d) in zip(t_beta, g_row, chains)]
    for i, (c, d) in enumerate(chains):
        u_ref[0, d, 0, rows[c], :] = us[i]
        w_ref[0, d, 0, rows[c], :] = ws[i].astype(BF16)
        qk_ref[0, d, 0, rows[c], :] = (qk_kk[c][:CHUNK] * decay[i]).astype(BF16)
        qd_ref[0, d, 0, rows[c], :] = (q[c].astype(F32) * jnp.exp(g_col[i])).astype(BF16)
        kd_ref[0, d, 0, rows[c], :] = (k[c].astype(F32) * jnp.exp(g_last[i] - g_col[i])).astype(BF16)


def _dn_local(qkv, grow, glast, *, batch, seq, heads):
    nc = seq // CHUNK
    cbk = _tile(nc, 32)
    L = cbk * CHUNK
    nb = nc // cbk
    tok = lambda off: pl.BlockSpec((L, DN_K), lambda b, h, t: (b * nb + t, off + h))
    gate = pl.BlockSpec((1, 2, 1, cbk, LANES), lambda b, h, t: (b, 0, h, t, 0))
    out = lambda width: pl.BlockSpec((1, 2, 1, L, width), lambda b, h, t: (b, 0, h, t, 0))
    shp = lambda width, dt: jax.ShapeDtypeStruct((batch, 2, heads, seq, width), dt)
    return pl.pallas_call(
        functools.partial(_dn_local_body, chunks=cbk),
        grid=(batch, heads, nb),
        in_specs=[tok(0), tok(heads), tok(2 * heads), gate, gate],
        out_specs=[out(DN_V), out(DN_K), out(DN_K), out(DN_K), out(CHUNK)],
        out_shape=[shp(DN_V, F32), shp(DN_K, BF16), shp(DN_K, BF16), shp(DN_K, BF16), shp(CHUNK, BF16)],
        compiler_params=_params(("parallel", "parallel", "parallel")),
        name="dn_local",
    )(qkv, qkv, qkv, grow, glast)


def _dn_scan_body(*refs, chunks, hb):
    ins, (of_ref, ob_ref, st_ref) = refs[:12], refs[12:]
    t = pl.program_id(2)

    @pl.when(t == 0)
    def _():
        st_ref[...] = jnp.zeros_like(st_ref)

    chains = [(d, j) for d in range(2) for j in range(hb)]
    states = [st_ref[d, j] for d, j in chains]
    for ci in range(chunks):
        blk = []
        for d, j in chains:
            u_ref, w_ref, qd_ref, kd_ref, qk_ref, gl_ref = ins[6 * d:6 * d + 6]
            c = ci if d == 0 else chunks - 1 - ci
            rows = slice(c * CHUNK, (c + 1) * CHUNK)
            blk.append((u_ref[0, 0, j, rows, :], w_ref[0, 0, j, rows, :], qd_ref[0, 0, j, rows, :],
                        kd_ref[0, 0, j, rows, :], qk_ref[0, 0, j, rows, :],
                        jnp.exp(gl_ref[0, 0, j, c:c + 1, :]), rows))
        sb = [s.astype(BF16) for s in states]
        w_s = [_dot(b[1], s) for b, s in zip(blk, sb)]
        q_s = [_dot(b[2], s) for b, s in zip(blk, sb)]
        vb = [(b[0] - x).astype(BF16) for b, x in zip(blk, w_s)]
        outs = [x + _dot(b[4], y) for b, x, y in zip(blk, q_s, vb)]
        states = [s * b[5] + _dot_tn(b[3], y) for b, s, y in zip(blk, states, vb)]
        for (d, j), b, o in zip(chains, blk, outs):
            o_ref = of_ref if d == 0 else ob_ref
            o_ref[b[6], j * DN_V:(j + 1) * DN_V] = o
    for (d, j), s in zip(chains, states):
        st_ref[d, j] = s


def _dn_scan(u, w, qd, kd, qk, glast, *, batch, seq, heads):
    nc = seq // CHUNK
    cbk = _tile(nc, 8)
    L = cbk * CHUNK
    nb = nc // cbk
    hb = _tile(heads, 8)

    def spec(d, rows, width):
        if d == 0:
            return pl.BlockSpec((1, 1, hb, rows, width), lambda b, g, t: (b, 0, g, t, 0))
        return pl.BlockSpec((1, 1, hb, rows, width), lambda b, g, t: (b, 1, g, nb - 1 - t, 0))

    in_specs, args = [], []
    for d in range(2):
        in_specs += [spec(d, L, DN_V), spec(d, L, DN_K), spec(d, L, DN_K), spec(d, L, DN_K),
                     spec(d, L, CHUNK), spec(d, cbk, LANES)]
        args += [u, w, qd, kd, qk, glast]
    out_shape = jax.ShapeDtypeStruct((batch * seq, heads * DN_V), F32)
    return pl.pallas_call(
        functools.partial(_dn_scan_body, chunks=cbk, hb=hb),
        grid=(batch, heads // hb, nb),
        in_specs=in_specs,
        out_specs=[pl.BlockSpec((L, hb * DN_V), lambda b, g, t: (b * nb + t, g)),
                   pl.BlockSpec((L, hb * DN_V), lambda b, g, t: (b * nb + nb - 1 - t, g))],
        out_shape=[out_shape, out_shape],
        scratch_shapes=[pltpu.VMEM((2, hb, DN_K, DN_V), F32)],
        compiler_params=_params(("parallel", "parallel", "arbitrary")),
        name="dn_scan",
    )(*args)


def _dn_out_body(of_ref, ob_ref, z_ref, g_ref, o_ref):
    o = of_ref[...] + ob_ref[...]
    z = z_ref[...]
    g = g_ref[...]
    for h in range(o.shape[1] // DN_V):
        sl = slice(h * DN_V, (h + 1) * DN_V)
        oh, zh = o[:, sl], z[:, sl]
        y = oh * lax.rsqrt(jnp.mean(oh * oh, axis=-1, keepdims=True) + EPS) * g
        o_ref[:, sl] = (y * (zh * jax.nn.sigmoid(zh))).astype(o_ref.dtype)


def _dn_out(o_f, o_b, p2, norm_g, *, z_col0):
    M, W = o_f.shape
    tm, tw = _tile(M, 512), _tile(W, 512)
    zj = z_col0 // tw
    return pl.pallas_call(
        _dn_out_body,
        grid=(M // tm, W // tw),
        in_specs=[pl.BlockSpec((tm, tw), lambda i, j: (i, j)),
                  pl.BlockSpec((tm, tw), lambda i, j: (i, j)),
                  pl.BlockSpec((tm, tw), lambda i, j: (i, zj + j)),
                  pl.BlockSpec((1, DN_V), lambda i, j: (0, 0))],
        out_specs=pl.BlockSpec((tm, tw), lambda i, j: (i, j)),
        out_shape=jax.ShapeDtypeStruct((M, W), BF16),
        compiler_params=_params(("parallel", "parallel")),
        name="dn_out",
    )(o_f, o_b, p2, norm_g.reshape(1, DN_V))


def _rope_table(positions):
    half = ROPE // 2
    inv_freq = ROPE_THETA ** (-jnp.arange(half, dtype=F32) / half)
    ang = positions.astype(F32)[..., None] * inv_freq
    cos, sin = jnp.cos(ang), jnp.sin(ang)
    return jnp.concatenate([cos, cos, sin, sin], axis=-1).reshape(-1, 4 * half)


def _mixer(u, tab, w_in, q_norm_g, kv_norm_g, w_uq, w_ukv, w_branch_a, conv_w, a_log, dt_bias,
           o_norm_g, w_branch_b, w_out, *, batch, seq, stream, gate, alpha):
    d_model = u.shape[1]
    q_rank, kv_rank = q_norm_g.shape[0], kv_norm_g.shape[0]
    mla_heads = w_uq.shape[1] // MLA_QK
    dn_heads = a_log.shape[1]
    dn_qk_w, dn_v_w = dn_heads * DN_K, dn_heads * DN_V
    c_qkv = q_rank + kv_rank + ROPE
    c_z = c_qkv + 2 * dn_qk_w + dn_v_w
    c_a = c_z + dn_v_w
    c_gate = c_a + 4 * dn_heads
    assert ROPE + 4 * dn_heads == LANES and c_qkv - ROPE == (q_rank + kv_rank)
    w1 = jnp.concatenate([w_in[:, :c_qkv], w_in[:, c_a:c_gate]], axis=1).astype(BF16)
    w2 = jnp.concatenate([w_in[:, c_qkv:c_a], w_in[:, c_gate:]], axis=1).astype(BF16)
    p1 = _matmul(u, w1, F32, tm=512, tn=w1.shape[1])
    p2 = _matmul(u, w2, F32)

    wq = w_uq.reshape(q_rank, mla_heads, MLA_QK)
    r1, r2 = wq[..., NOPE:NOPE + ROPE // 2], wq[..., NOPE + ROPE // 2:]
    wq = jnp.concatenate([wq, -r2, r1], axis=-1).transpose(1, 0, 2).astype(BF16)
    wkv = w_ukv.reshape(kv_rank, mla_heads, NOPE + MLA_V).transpose(1, 0, 2).astype(BF16)
    q, k, v = _mla_prep(p1, tab, q_norm_g, kv_norm_g, wq, wkv, batch=batch, seq=seq)
    o_a = _flash(q, k, v).reshape(batch * seq, mla_heads * MLA_V)

    qkv = _dn_conv(p2, conv_w, batch=batch, seq=seq, heads=dn_heads)
    gates = _dn_gates(p1, a_log, dt_bias, batch=batch, seq=seq, heads=dn_heads,
                      col_block=(q_rank + kv_rank) // LANES)
    nc = seq // CHUNK
    gates = gates.reshape(batch, 2, 2, dn_heads, nc, CHUNK)
    grow = gates.transpose(0, 2, 3, 4, 1, 5).reshape(batch, 2, dn_heads, nc, 2 * CHUNK)
    g_cum = gates[:, 0]
    glast = jnp.stack([g_cum[:, 0, :, :, CHUNK - 1], g_cum[:, 1, :, :, 0]], axis=1)
    glast = jnp.broadcast_to(glast[..., None], glast.shape + (LANES,))
    u_, w_, qd, kd, qk = _dn_local(qkv, grow, glast, batch=batch, seq=seq, heads=dn_heads)
    o_f, o_b = _dn_scan(u_, w_, qd, kd, qk, glast, batch=batch, seq=seq, heads=dn_heads)
    o_bn = _dn_out(o_f, o_b, p2, o_norm_g, z_col0=2 * dn_qk_w + dn_v_w)

    merged = _merge(o_a, o_bn, w_branch_a.astype(BF16), w_branch_b.astype(BF16), p2, d_model,
                    gate_col0=2 * dn_qk_w + 2 * dn_v_w)
    return _matmul(merged, w_out.astype(BF16), F32, stream=stream, gate=gate, seq=seq,
                   residual=(alpha, 1.0), columns_outer=True)


def _ffn(h, w_gate, w_up, w_down, layer, *, stream, gate, seq, alpha):
    a = _gateup(h, w_gate, w_up, layer)
    return _matmul(a, w_down[layer].astype(BF16), F32, tm=512, tn=1024, tk=w_down.shape[1],
                   stream=stream, gate=gate, seq=seq, residual=(alpha, 0.5), columns_outer=True)


def kernel(x, c, positions, ln_in_g, ln_in_b, w_ada, b_ada, ada_table, ffn1_w_gate, ffn1_w_up, ffn1_w_down, w_in, mla_q_norm_g, mla_kv_norm_g, mla_w_uq, mla_w_ukv, w_branch_a, dn_conv_w, dn_a_log, dn_dt_bias, dn_norm_g, w_branch_b, w_out, ffn2_w_gate, ffn2_w_up, ffn2_w_down, post_ln_g, post_ln_b):
    B, S, D = x.shape
    depth = ada_table.shape[0]
    alpha = (2.0 * depth) ** 0.25
    cond = _adaln(c, w_ada, b_ada).reshape(B, N_SUB, 3, D)
    mod = cond[None] + ada_table[:, None]
    tab = _rope_table(positions)
    post = functools.partial(_post, seq=S)

    xs, g, b = x.reshape(B * S, D), ln_in_g, ln_in_b
    h, stats = post(xs, g, b, mod[0, :, 0, 0], mod[0, :, 0, 1])
    for l in range(depth):
        xs = _ffn(h, ffn1_w_gate, ffn1_w_up, ffn1_w_down, l, stream=(xs, stats, g, b), gate=mod[l, :, 0, 2],
                  seq=S, alpha=alpha)
        g, b = post_ln_g[l, 0], post_ln_b[l, 0]
        h, stats = post(xs, g, b, mod[l, :, 1, 0], mod[l, :, 1, 1])
        xs = _mixer(h, tab, w_in[l], mla_q_norm_g[l], mla_kv_norm_g[l], mla_w_uq[l], mla_w_ukv[l],
                    w_branch_a[l], dn_conv_w[l], dn_a_log[l], dn_dt_bias[l], dn_norm_g[l],
                    w_branch_b[l], w_out[l], batch=B, seq=S, stream=(xs, stats, g, b), gate=mod[l, :, 1, 2],
                    alpha=alpha)
        g, b = post_ln_g[l, 1], post_ln_b[l, 1]
        h, stats = post(xs, g, b, mod[l, :, 2, 0], mod[l, :, 2, 1])
        xs = _ffn(h, ffn2_w_gate, ffn2_w_up, ffn2_w_down, l, stream=(xs, stats, g, b), gate=mod[l, :, 2, 2],
                  seq=S, alpha=alpha)
        g, b = post_ln_g[l, 2], post_ln_b[l, 2]
        if l + 1 < depth:
            h, stats = post(xs, g, b, mod[l + 1, :, 0, 0], mod[l + 1, :, 0, 1])
    return post(xs, g, b, None, None).reshape(B, S, D)
```

```python
import functools
import math

import jax
import jax.numpy as jnp
from jax import lax
from jax.experimental import pallas as pl
from jax.experimental.pallas import tpu as pltpu

F32 = jnp.float32
BF16 = jnp.bfloat16

NOPE = 128
ROPE = 64
MLA_V = 128
MLA_QK = NOPE + ROPE
DN_K = 128
DN_V = 128
DN_CONV = 5
CHUNK = 64
N_SUB = 3
EPS = 1e-6
ROPE_THETA = 10000.0
LOG2E = math.log2(math.e)

LANES = 128
SUBLANES = 8
V7X_VMEM_LIMIT_BYTES = 56 * 1024 * 1024


def _tile(dim, pref):
    t = min(dim, pref)
    assert dim % t == 0, (dim, pref)
    return t


def _params(sem):
    return pltpu.CompilerParams(dimension_semantics=sem, vmem_limit_bytes=V7X_VMEM_LIMIT_BYTES)


def _dot(a, b):
    return jnp.dot(a, b, preferred_element_type=F32)


def _dot_nt(a, b):
    return lax.dot_general(a, b, (((1,), (1,)), ((), ())), preferred_element_type=F32)


def _dot_tn(a, b):
    return lax.dot_general(a, b, (((0,), (0,)), ((), ())), preferred_element_type=F32)


def _cast_body(w_ref, o_ref):
    o_ref[...] = w_ref[0].astype(o_ref.dtype)


def _cast_layer(w, layer):
    _, K, N = w.shape
    tr = _tile(K, 512)
    return pl.pallas_call(
        _cast_body,
        grid=(K // tr,),
        in_specs=[pl.BlockSpec((1, tr, N), lambda i: (layer, i, 0))],
        out_specs=pl.BlockSpec((tr, N), lambda i: (i, 0)),
        out_shape=jax.ShapeDtypeStruct((K, N), BF16),
        compiler_params=_params(("parallel",)),
        name="weight_cast",
    )(w)


def _adaln_body(c_ref, w_ref, b_ref, o_ref):
    c = c_ref[...]
    h = (c * jax.nn.sigmoid(c)).astype(BF16)
    o_ref[...] = _dot(h, w_ref[...].astype(BF16)) + b_ref[...]


def _adaln(c, w_ada, b_ada):
    B, D = c.shape
    N = w_ada.shape[1]
    rows = max(SUBLANES, B)
    c_pad = jnp.zeros((rows, D), F32).at[:B].set(c)
    tn = _tile(N, 512)
    out = pl.pallas_call(
        _adaln_body,
        grid=(N // tn,),
        in_specs=[pl.BlockSpec((rows, D), lambda j: (0, 0)),
                  pl.BlockSpec((D, tn), lambda j: (0, j)),
                  pl.BlockSpec((1, tn), lambda j: (0, j))],
        out_specs=pl.BlockSpec((rows, tn), lambda j: (0, j)),
        out_shape=jax.ShapeDtypeStruct((rows, N), F32),
        compiler_params=_params(("parallel",)),
        name="adaln",
    )(c_pad, w_ada, b_ada.reshape(1, N))
    return out[:B]


def _post_body(*refs, has_mod):
    it = iter(refs)
    x_ref, g_ref, b_ref = next(it), next(it), next(it)
    if has_mod:
        sh_ref, sc_ref, h_ref, st_ref = next(it), next(it), next(it), next(it)
    else:
        xo_ref = next(it)
    x = x_ref[...]
    mu = jnp.mean(x, axis=-1, keepdims=True)
    xc = x - mu
    var = jnp.mean(xc * xc, axis=-1, keepdims=True)
    rstd = lax.rsqrt(var + EPS)
    y = xc * rstd * g_ref[...] + b_ref[...]
    if has_mod:
        h_ref[...] = (y * (1.0 + sc_ref[0]) + sh_ref[0]).astype(BF16)
        st_ref[:, :LANES] = jnp.broadcast_to(mu, (mu.shape[0], LANES))
        st_ref[:, LANES:] = jnp.broadcast_to(rstd, (rstd.shape[0], LANES))
    else:
        xo_ref[...] = y


def _post(x, g, b, shift, scale, *, seq):
    M, D = x.shape
    has_mod = shift is not None
    tm = _tile(seq, 256)
    per_b = seq // tm
    row = pl.BlockSpec((tm, D), lambda i: (i, 0))
    vec = pl.BlockSpec((1, D), lambda i: (0, 0))
    bvec = pl.BlockSpec((1, 1, D), lambda i: (i // per_b, 0, 0))
    args = [x, g.reshape(1, D), b.reshape(1, D)]
    specs = [row, vec, vec]
    if has_mod:
        args += [shift.reshape(-1, 1, D), scale.reshape(-1, 1, D)]
        specs += [bvec, bvec]
        out_shape = [jax.ShapeDtypeStruct((M, D), BF16), jax.ShapeDtypeStruct((M, 2 * LANES), F32)]
        out_specs = [row, pl.BlockSpec((tm, 2 * LANES), lambda i: (i, 0))]
    else:
        out_shape = [jax.ShapeDtypeStruct((M, D), F32)]
        out_specs = [row]
    outs = pl.pallas_call(
        functools.partial(_post_body, has_mod=has_mod),
        grid=(M // tm,),
        in_specs=specs,
        out_specs=out_specs,
        out_shape=out_shape,
        compiler_params=_params(("parallel",)),
        name="post_ln",
    )(*args)
    return outs if has_mod else outs[0]


def _mm_body(*refs, nk, residual):
    a_ref, w_ref = refs[:2]
    refs = refs[2:]
    if residual is not None:
        (x_ref, st_ref, lg_ref, lb_ref, gate_ref), refs = refs[:5], refs[5:]
    o_ref = refs[0]

    def finish(acc):
        if residual is None:
            o_ref[...] = acc.astype(o_ref.dtype)
            return
        alpha, r = residual
        mu, rstd = st_ref[:, :LANES], st_ref[:, LANES:]
        for s in range(acc.shape[1] // LANES):
            sl = slice(s * LANES, (s + 1) * LANES)
            xn = (x_ref[:, sl] - mu) * rstd * lg_ref[:, sl] + lb_ref[:, sl]
            o_ref[:, sl] = alpha * xn + (r * gate_ref[0][:, sl]) * acc[:, sl]

    if nk == 1:
        finish(_dot(a_ref[...], w_ref[...]))
        return
    acc_ref = refs[1]
    k = pl.program_id(2)

    @pl.when(k == 0)
    def _():
        acc_ref[...] = jnp.zeros_like(acc_ref)

    acc_ref[...] += _dot(a_ref[...], w_ref[...])

    @pl.when(k == nk - 1)
    def _():
        finish(acc_ref[...])


def _matmul(a, w, out_dtype, *, tm=1024, tn=1024, tk=4096, stream=None, gate=None, seq=None, residual=None,
            columns_outer=False):
    M, K = a.shape
    N = w.shape[1]
    tm, tn, tk = _tile(M if residual is None else seq, tm), _tile(N, tn), _tile(K, tk)
    nk = K // tk
    if columns_outer:
        assert nk == 1
        grid = (N // tn, M // tm, nk)
        at = lambda f: (lambda j, i, k: f(i, j, k))
        w_spec = pl.BlockSpec((tk, tn), at(lambda i, j, k: (k, j)), pipeline_mode=pl.Buffered(1))
    else:
        grid = (M // tm, N // tn, nk)
        at = lambda f: f
        w_spec = pl.BlockSpec((tk, tn), lambda i, j, k: (k, j))
    args = [a, w]
    in_specs = [pl.BlockSpec((tm, tk), at(lambda i, j, k: (i, k))), w_spec]
    if residual is not None:
        per_b = seq // tm
        x, stats, g, b = stream
        args += [x, stats, g.reshape(1, N), b.reshape(1, N), gate.reshape(-1, 1, N)]
        in_specs += [pl.BlockSpec((tm, tn), at(lambda i, j, k: (i, j))),
                     pl.BlockSpec((tm, 2 * LANES), at(lambda i, j, k: (i, 0))),
                     pl.BlockSpec((1, tn), at(lambda i, j, k: (0, j))),
                     pl.BlockSpec((1, tn), at(lambda i, j, k: (0, j))),
                     pl.BlockSpec((1, 1, tn), at(lambda i, j, k: (i // per_b, 0, j)))]
    return pl.pallas_call(
        functools.partial(_mm_body, nk=nk, residual=residual),
        grid=grid,
        in_specs=in_specs,
        out_specs=pl.BlockSpec((tm, tn), at(lambda i, j, k: (i, j))),
        out_shape=jax.ShapeDtypeStruct((M, N), out_dtype),
        scratch_shapes=[pltpu.VMEM((tm, tn), F32)] if nk > 1 else [],
        compiler_params=_params(("parallel", "parallel", "arbitrary")),
        name="matmul",
    )(*args)


def _gateup_body(a_ref, wg_ref, wu_ref, o_ref, wgb_ref, wub_ref):
    @pl.when(pl.program_id(1) == 0)
    def _():
        wgb_ref[...] = wg_ref[0].astype(BF16)
        wub_ref[...] = wu_ref[0].astype(BF16)

    a = a_ref[...]
    g = _dot(a, wgb_ref[...])
    u = _dot(a, wub_ref[...])
    o_ref[...] = (g * jax.nn.sigmoid(g) * u).astype(o_ref.dtype)


def _gateup(a, wg, wu, layer):
    M, K = a.shape
    N = wg.shape[2]
    tm, tn = _tile(M, 1024), _tile(N, 512)
    weight = pl.BlockSpec((1, K, tn), lambda j, i: (layer, 0, j), pipeline_mode=pl.Buffered(1))
    return pl.pallas_call(
        _gateup_body,
        grid=(N // tn, M // tm),
        in_specs=[pl.BlockSpec((tm, K), lambda j, i: (i, 0)), weight, weight],
        out_specs=pl.BlockSpec((tm, tn), lambda j, i: (i, j)),
        out_shape=jax.ShapeDtypeStruct((M, N), BF16),
        scratch_shapes=[pltpu.VMEM((K, tn), BF16), pltpu.VMEM((K, tn), BF16)],
        compiler_params=_params(("parallel", "arbitrary")),
        name="ffn_gate_up",
    )(a, wg, wu)


def _merge_body(oa_ref, ob_ref, wa_ref, wb_ref, ga_ref, gb_ref, o_ref):
    ya = _dot(oa_ref[...], wa_ref[...])
    yb = _dot(ob_ref[...], wb_ref[...])
    o_ref[...] = (jax.nn.sigmoid(ga_ref[...]) * ya + jax.nn.sigmoid(gb_ref[...]) * yb).astype(o_ref.dtype)


def _merge(oa, ob, wa, wb, p2, d_model, gate_col0):
    M, Ka = oa.shape
    Kb = ob.shape[1]
    tm, tn = _tile(M, 1024), _tile(d_model, 1024)
    ja = gate_col0 // tn
    jb = (gate_col0 + d_model) // tn
    return pl.pallas_call(
        _merge_body,
        grid=(d_model // tn, M // tm),
        in_specs=[pl.BlockSpec((tm, Ka), lambda j, i: (i, 0)),
                  pl.BlockSpec((tm, Kb), lambda j, i: (i, 0)),
                  pl.BlockSpec((Ka, tn), lambda j, i: (0, j), pipeline_mode=pl.Buffered(1)),
                  pl.BlockSpec((Kb, tn), lambda j, i: (0, j), pipeline_mode=pl.Buffered(1)),
                  pl.BlockSpec((tm, tn), lambda j, i: (i, ja + j)),
                  pl.BlockSpec((tm, tn), lambda j, i: (i, jb + j))],
        out_specs=pl.BlockSpec((tm, tn), lambda j, i: (i, j)),
        out_shape=jax.ShapeDtypeStruct((M, d_model), BF16),
        compiler_params=_params(("parallel", "parallel")),
        name="branch_merge",
    )(oa, ob, wa, wb, p2, p2)


def _rms(x, g):
    return x * lax.rsqrt(jnp.mean(x * x, axis=-1, keepdims=True) + EPS) * g


def _mla_prep_body(ql_ref, kvl_ref, kr_ref, tab_ref, gq_ref, gkv_ref, wq_ref, wkv_ref,
                   q_ref, k_ref, v_ref, *, heads, q_scale):
    qn = _rms(ql_ref[...], gq_ref[...]).astype(BF16)
    kvn = _rms(kvl_ref[...], gkv_ref[...]).astype(BF16)
    tab = tab_ref[...]
    tab_sin = pltpu.roll(tab, 64, 1)
    x = kr_ref[...]
    lane = lax.broadcasted_iota(jnp.int32, x.shape, 1)
    x_rot = jnp.where(lane < ROPE // 2, -pltpu.roll(x, LANES - ROPE // 2, 1), pltpu.roll(x, ROPE // 2, 1))
    k_rope_t = (x * tab + x_rot * tab_sin).T[:ROPE].astype(BF16)
    for h in range(heads):
        r = _dot(qn, wq_ref[h])
        p = r[:, NOPE:] * tab
        roped = p + pltpu.roll(p, 64, 1)
        q_ref[0, h, :, :NOPE] = (r[:, :NOPE] * q_scale).astype(BF16)
        q_ref[0, h, :, NOPE:] = (roped[:, :ROPE] * q_scale).astype(BF16)
        kv = _dot(kvn, wkv_ref[h])
        k_ref[0, h, :NOPE, :] = kv[:, :NOPE].T.astype(BF16)
        k_ref[0, h, NOPE:, :] = k_rope_t
        v_ref[0, h] = kv[:, NOPE:].astype(BF16)


def _mla_prep(p1, tab, gq, gkv, wq, wkv, *, batch, seq):
    heads, q_rank, _ = wq.shape
    kv_rank = wkv.shape[1]
    ts = _tile(seq, 512)
    ns = seq // ts
    row = lambda b, i: b * ns + i
    return pl.pallas_call(
        functools.partial(_mla_prep_body, heads=heads, q_scale=MLA_QK ** -0.5 * LOG2E),
        grid=(batch, ns),
        in_specs=[pl.BlockSpec((ts, q_rank), lambda b, i: (row(b, i), 0)),
                  pl.BlockSpec((ts, kv_rank), lambda b, i: (row(b, i), q_rank // kv_rank)),
                  pl.BlockSpec((ts, LANES), lambda b, i: (row(b, i), (q_rank + kv_rank) // LANES)),
                  pl.BlockSpec((ts, LANES), lambda b, i: (row(b, i), 0)),
                  pl.BlockSpec((1, q_rank), lambda b, i: (0, 0)),
                  pl.BlockSpec((1, kv_rank), lambda b, i: (0, 0)),
                  pl.BlockSpec((heads, q_rank, 2 * LANES), lambda b, i: (0, 0, 0)),
                  pl.BlockSpec((heads, kv_rank, 2 * LANES), lambda b, i: (0, 0, 0))],
        out_specs=[pl.BlockSpec((1, heads, ts, MLA_QK), lambda b, i: (b, 0, i, 0)),
                   pl.BlockSpec((1, heads, MLA_QK, ts), lambda b, i: (b, 0, 0, i)),
                   pl.BlockSpec((1, heads, ts, MLA_V), lambda b, i: (b, 0, i, 0))],
        out_shape=[jax.ShapeDtypeStruct((batch, heads, seq, MLA_QK), BF16),
                   jax.ShapeDtypeStruct((batch, heads, MLA_QK, seq), BF16),
                   jax.ShapeDtypeStruct((batch, heads, seq, MLA_V), BF16)],
        compiler_params=_params(("parallel", "parallel")),
        name="mla_prep",
    )(p1, p1, p1, tab, gq.reshape(1, -1), gkv.reshape(1, -1), wq, wkv)


def _flash_body(q_ref, kt_ref, v_ref, o_ref, s_ref, m_ref, l_ref, acc_ref, *, tk, nk):
    m_ref[...] = jnp.full_like(m_ref, -jnp.inf)
    l_ref[...] = jnp.zeros_like(l_ref)
    acc_ref[...] = jnp.zeros_like(acc_ref)
    q = q_ref[0, 0]

    def key_rows(j):
        return slice(j * tk, (j + 1) * tk)

    def scores(j):
        return _dot(q, kt_ref[0, 0, :, key_rows(j)])

    s_ref[0] = scores(0)
    for j in range(nk):
        s = s_ref[j % 2]
        if j + 1 < nk:
            s_ref[(j + 1) % 2] = scores(j + 1)
        m_prev = m_ref[...]
        m_new = jnp.maximum(m_prev, jnp.max(s, axis=1, keepdims=True))
        alpha = jnp.exp2(m_prev - m_new)
        p = jnp.exp2(s - m_new[:, :1])
        l_ref[...] = alpha * l_ref[...] + jnp.sum(p, axis=1, keepdims=True)
        acc_ref[...] = alpha * acc_ref[...] + _dot(p.astype(BF16), v_ref[0, 0, key_rows(j), :])
        m_ref[...] = m_new
    o_ref[0] = (acc_ref[...] / l_ref[...]).astype(o_ref.dtype)


def _flash(q, k, v):
    B, H, S, _ = q.shape
    tq, tk = _tile(S, 1024), _tile(S, 2048)
    return pl.pallas_call(
        functools.partial(_flash_body, tk=tk, nk=S // tk),
        grid=(B, H, S // tq),
        in_specs=[pl.BlockSpec((1, 1, tq, MLA_QK), lambda b, h, i: (b, h, i, 0)),
                  pl.BlockSpec((1, 1, MLA_QK, S), lambda b, h, i: (b, h, 0, 0)),
                  pl.BlockSpec((1, 1, S, MLA_V), lambda b, h, i: (b, h, 0, 0))],
        out_specs=pl.BlockSpec((1, tq, MLA_V), lambda b, h, i: (b, i, h)),
        out_shape=jax.ShapeDtypeStruct((B, S, H * MLA_V), BF16),
        scratch_shapes=[pltpu.VMEM((2, tq, tk), F32), pltpu.VMEM((tq, LANES), F32),
                        pltpu.VMEM((tq, LANES), F32), pltpu.VMEM((tq, MLA_V), F32)],
        compiler_params=_params(("parallel", "parallel", "arbitrary")),
        name="mla_flash",
    )(q, k, v)


def _dn_conv_body(cur_ref, prev_ref, next_ref, w_ref, o_ref, buf_ref, *, ts, ns, qk_blocks):
    i = pl.program_id(1)
    c = pl.program_id(2)
    pad = DN_CONV // 2
    buf_ref[0:SUBLANES] = jnp.where(i == 0, 0.0, prev_ref[...])
    buf_ref[SUBLANES:SUBLANES + ts] = cur_ref[...]
    buf_ref[SUBLANES + ts:] = jnp.where(i == ns - 1, 0.0, next_ref[...])
    w = w_ref[...]
    full = buf_ref[...]
    rows = full.shape[0]
    acc = w[pad:pad + 1] * full[SUBLANES:SUBLANES + ts]
    for t in range(DN_CONV):
        if t != pad:
            acc = acc + w[t:t + 1] * pltpu.roll(full, (pad - t) % rows, 0)[SUBLANES:SUBLANES + ts]
    y = acc * jax.nn.sigmoid(acc)
    kind = c // qk_blocks
    for g in range(y.shape[1] // DN_K):
        yg = y[:, g * DN_K:(g + 1) * DN_K]
        inv = lax.rsqrt(jnp.sum(yg * yg, axis=-1, keepdims=True) + EPS)
        inv = jnp.where(kind < 2, inv, 1.0) * jnp.where(kind == 0, DN_K ** -0.5, 1.0)
        o_ref[:, g * DN_K:(g + 1) * DN_K] = (yg * inv).astype(o_ref.dtype)


def _dn_conv(p2, conv_w, *, batch, seq, heads):
    width = 3 * heads * DN_K
    ts = _tile(seq, 512)
    ns = seq // ts
    cb = 4 * DN_K
    qk_blocks = heads * DN_K // cb
    r8 = ts // SUBLANES
    last8 = batch * seq // SUBLANES - 1
    return pl.pallas_call(
        functools.partial(_dn_conv_body, ts=ts, ns=ns, qk_blocks=qk_blocks),
        grid=(batch, ns, width // cb),
        in_specs=[pl.BlockSpec((ts, cb), lambda b, i, c: (b * ns + i, c)),
                  pl.BlockSpec((SUBLANES, cb), lambda b, i, c: (jnp.maximum((b * ns + i) * r8 - 1, 0), c)),
                  pl.BlockSpec((SUBLANES, cb), lambda b, i, c: (jnp.minimum((b * ns + i + 1) * r8, last8), c)),
                  pl.BlockSpec((DN_CONV, cb), lambda b, i, c: (0, c))],
        out_specs=pl.BlockSpec((ts, cb), lambda b, i, c: (b * ns + i, c)),
        out_shape=jax.ShapeDtypeStruct((batch * seq, width), BF16),
        scratch_shapes=[pltpu.VMEM((ts + 2 * SUBLANES, cb), F32)],
        compiler_params=_params(("parallel", "parallel", "parallel")),
        name="dn_conv",
    )(p2, p2, p2, conv_w)


def _dn_gate_body(x_ref, alog_ref, dtb_ref, o_ref, *, heads):
    x = x_ref[...]
    z = x + dtb_ref[...]
    softplus = jnp.maximum(z, 0.0) + jnp.log1p(jnp.exp(-jnp.abs(z)))
    g = -jnp.exp(alog_ref[...]) * softplus
    lane = lax.broadcasted_iota(jnp.int32, x.shape, 1)
    zt = jnp.where(lane < LANES - 2 * heads, g, jax.nn.sigmoid(x)).T
    o_ref[0, 2 * heads:] = zt[LANES - 2 * heads:]
    pos = lax.broadcasted_iota(jnp.int32, (2 * heads, LANES), 1) % CHUNK
    fwd = lax.broadcasted_iota(jnp.int32, (2 * heads, LANES), 0) < heads
    for grp in range(zt.shape[1] // LANES):
        cols = slice(grp * LANES, (grp + 1) * LANES)
        gt = zt[LANES - 4 * heads:LANES - 2 * heads, cols]
        pre, suf = gt, gt
        s = 1
        while s < CHUNK:
            pre = pre + jnp.where(pos >= s, pltpu.roll(pre, s, 1), 0.0)
            suf = suf + jnp.where(pos < CHUNK - s, pltpu.roll(suf, LANES - s, 1), 0.0)
            s *= 2
        o_ref[0, :2 * heads, cols] = jnp.where(fwd, pre, suf)


def _dn_gates(p1, a_log, dt_bias, *, batch, seq, heads, col_block):
    ts = _tile(seq, 512)
    ns = seq // ts
    lo, hi = LANES - 4 * heads, LANES - 2 * heads
    alog = jnp.zeros((1, LANES), F32).at[0, lo:hi].set(a_log.reshape(-1))
    dtb = jnp.zeros((1, LANES), F32).at[0, lo:hi].set(dt_bias.reshape(-1))
    return pl.pallas_call(
        functools.partial(_dn_gate_body, heads=heads),
        grid=(batch, ns),
        in_specs=[pl.BlockSpec((ts, LANES), lambda b, i: (b * ns + i, col_block)),
                  pl.BlockSpec((1, LANES), lambda b, i: (0, 0)),
                  pl.BlockSpec((1, LANES), lambda b, i: (0, 0))],
        out_specs=pl.BlockSpec((1, 4 * heads, ts), lambda b, i: (b, 0, i)),
        out_shape=jax.ShapeDtypeStruct((batch, 4 * heads, seq), F32),
        compiler_params=_params(("parallel", "parallel")),
        name="dn_gates",
    )(p1, alog, dtb)


def _split_dot(a, b):
    ah, bh = a.astype(BF16), b.astype(BF16)
    al = (a - ah.astype(F32)).astype(BF16)
    bl = (b - bh.astype(F32)).astype(BF16)
    return _dot(ah, bh) + _dot(ah, bl) + _dot(al, bh)


def _unit_triangular_inverses(ms, eye):
    a = [eye + m for m in ms]
    ab = [x.astype(BF16) for x in a]
    t = [eye - m for m in ms]
    for _ in range(int(math.log2(CHUNK)) - 2):
        r = [2.0 * eye - _dot(x, y.astype(BF16)) for x, y in zip(ab, t)]
        t = [_dot(x.astype(BF16), y.astype(BF16)) for x, y in zip(t, r)]
    r = [eye - _split_dot(x, y) for x, y in zip(a, t)]
    return [x + _dot(x.astype(BF16), y.astype(BF16)) for x, y in zip(t, r)]


def _dn_local_body(q_ref, k_ref, v_ref, grow_ref, glast_ref,
                   u_ref, w_ref, qd_ref, kd_ref, qk_ref, *, chunks):
    ii = lax.broadcasted_iota(jnp.int32, (CHUNK, CHUNK), 0)
    jj = lax.broadcasted_iota(jnp.int32, (CHUNK, CHUNK), 1)
    eye = (ii == jj).astype(F32)
    rows = [slice(c * CHUNK, (c + 1) * CHUNK) for c in range(chunks)]
    q = [q_ref[r, :] for r in rows]
    k = [k_ref[r, :] for r in rows]
    v = [v_ref[r, :] for r in rows]
    qk_kk = [_dot_nt(jnp.concatenate([x, y], axis=0), y) for x, y in zip(q, k)]

    chains = [(c, d) for c in range(chunks) for d in range(2)]
    g_row, b_row, g_col, g_last, decay, ms = [], [], [], [], [], []
    for c, d in chains:
        incl = (jj <= ii) if d == 0 else (jj >= ii)
        strict = (jj < ii) if d == 0 else (jj > ii)
        r = jnp.broadcast_to(grow_ref[0, d, 0, c:c + 1, :], (LANES, LANES))
        rt = r.T
        gc, bc = rt[:CHUNK], rt[CHUNK:]
        gr = r[:CHUNK, :CHUNK]
        dec = jnp.exp(jnp.where(incl, gc[:, :CHUNK] - gr, -jnp.inf))
        ms.append(jnp.where(strict, qk_kk[c][CHUNK:] * bc[:, :CHUNK] * dec, 0.0))
        g_row.append(gr)
        b_row.append(r[:CHUNK, CHUNK:])
        g_col.append(gc)
        g_last.append(jnp.broadcast_to(glast_ref[0, d, 0, c:c + 1, :], (CHUNK, LANES)))
        decay.append(dec)

    t_beta = [t * b for t, b in zip(_unit_triangular_inverses(ms, eye), b_row)]
    us = [_dot(t.astype(BF16), v[c]) for t, (c, d) in zip(t_beta, chains)]
    ws = [_dot((t * jnp.exp(g)).astype(BF16), k[c]) for t, g, (c, d) in zip(t_beta, g_row, chains)]
    for i, (c, d) in enumerate(chains):
        u_ref[0, d, 0, rows[c], :] = us[i]
        w_ref[0, d, 0, rows[c], :] = ws[i].astype(BF16)
        qk_ref[0, d, 0, rows[c], :] = (qk_kk[c][:CHUNK] * decay[i]).astype(BF16)
        qd_ref[0, d, 0, rows[c], :] = (q[c].astype(F32) * jnp.exp(g_col[i])).astype(BF16)
        kd_ref[0, d, 0, rows[c], :] = (k[c].astype(F32) * jnp.exp(g_last[i] - g_col[i])).astype(BF16)


def _dn_local(qkv, grow, glast, *, batch, seq, heads):
    nc = seq // CHUNK
    cbk = _tile(nc, 32)
    L = cbk * CHUNK
    nb = nc // cbk
    tok = lambda off: pl.BlockSpec((L, DN_K), lambda b, h, t: (b * nb + t, off + h))
    gate = pl.BlockSpec((1, 2, 1, cbk, LANES), lambda b, h, t: (b, 0, h, t, 0))
    out = lambda width: pl.BlockSpec((1, 2, 1, L, width), lambda b, h, t: (b, 0, h, t, 0))
    shp = lambda width, dt: jax.ShapeDtypeStruct((batch, 2, heads, seq, width), dt)
    return pl.pallas_call(
        functools.partial(_dn_local_body, chunks=cbk),
        grid=(batch, heads, nb),
        in_specs=[tok(0), tok(heads), tok(2 * heads), gate, gate],
        out_specs=[out(DN_V), out(DN_K), out(DN_K), out(DN_K), out(CHUNK)],
        out_shape=[shp(DN_V, F32), shp(DN_K, BF16), shp(DN_K, BF16), shp(DN_K, BF16), shp(CHUNK, BF16)],
        compiler_params=_params(("parallel", "parallel", "parallel")),
        name="dn_local",
    )(qkv, qkv, qkv, grow, glast)


def _dn_scan_body(*refs, chunks, hb):
    ins, (of_ref, ob_ref, st_ref) = refs[:12], refs[12:]
    t = pl.program_id(2)

    @pl.when(t == 0)
    def _():
        st_ref[...] = jnp.zeros_like(st_ref)

    chains = [(d, j) for d in range(2) for j in range(hb)]
    states = [st_ref[d, j] for d, j in chains]
    for ci in range(chunks):
        blk = []
        for d, j in chains:
            u_ref, w_ref, qd_ref, kd_ref, qk_ref, gl_ref = ins[6 * d:6 * d + 6]
            c = ci if d == 0 else chunks - 1 - ci
            rows = slice(c * CHUNK, (c + 1) * CHUNK)
            blk.append((u_ref[0, 0, j, rows, :], w_ref[0, 0, j, rows, :], qd_ref[0, 0, j, rows, :],
                        kd_ref[0, 0, j, rows, :], qk_ref[0, 0, j, rows, :],
                        jnp.exp(gl_ref[0, 0, j, c:c + 1, :]), rows))
        sb = [s.astype(BF16) for s in states]
        w_s = [_dot(b[1], s) for b, s in zip(blk, sb)]
        q_s = [_dot(b[2], s) for b, s in zip(blk, sb)]
        vb = [(b[0] - x).astype(BF16) for b, x in zip(blk, w_s)]
        outs = [x + _dot(b[4], y) for b, x, y in zip(blk, q_s, vb)]
        states = [s * b[5] + _dot_tn(b[3], y) for b, s, y in zip(blk, states, vb)]
        for (d, j), b, o in zip(chains, blk, outs):
            o_ref = of_ref if d == 0 else ob_ref
            o_ref[b[6], j * DN_V:(j + 1) * DN_V] = o
    for (d, j), s in zip(chains, states):
        st_ref[d, j] = s


def _dn_scan(u, w, qd, kd, qk, glast, *, batch, seq, heads):
    nc = seq // CHUNK
    cbk = _tile(nc, 8)
    L = cbk * CHUNK
    nb = nc // cbk
    hb = _tile(heads, 8)

    def spec(d, rows, width):
        if d == 0:
            return pl.BlockSpec((1, 1, hb, rows, width), lambda b, g, t: (b, 0, g, t, 0))
        return pl.BlockSpec((1, 1, hb, rows, width), lambda b, g, t: (b, 1, g, nb - 1 - t, 0))

    in_specs, args = [], []
    for d in range(2):
        in_specs += [spec(d, L, DN_V), spec(d, L, DN_K), spec(d, L, DN_K), spec(d, L, DN_K),
                     spec(d, L, CHUNK), spec(d, cbk, LANES)]
        args += [u, w, qd, kd, qk, glast]
    out_shape = jax.ShapeDtypeStruct((batch * seq, heads * DN_V), F32)
    return pl.pallas_call(
        functools.partial(_dn_scan_body, chunks=cbk, hb=hb),
        grid=(batch, heads // hb, nb),
        in_specs=in_specs,
        out_specs=[pl.BlockSpec((L, hb * DN_V), lambda b, g, t: (b * nb + t, g)),
                   pl.BlockSpec((L, hb * DN_V), lambda b, g, t: (b * nb + nb - 1 - t, g))],
        out_shape=[out_shape, out_shape],
        scratch_shapes=[pltpu.VMEM((2, hb, DN_K, DN_V), F32)],
        compiler_params=_params(("parallel", "parallel", "arbitrary")),
        name="dn_scan",
    )(*args)


def _dn_out_body(of_ref, ob_ref, z_ref, g_ref, o_ref):
    o = of_ref[...] + ob_ref[...]
    z = z_ref[...]
    g = g_ref[...]
    for h in range(o.shape[1] // DN_V):
        sl = slice(h * DN_V, (h + 1) * DN_V)
        oh, zh = o[:, sl], z[:, sl]
        y = oh * lax.rsqrt(jnp.mean(oh * oh, axis=-1, keepdims=True) + EPS) * g
        o_ref[:, sl] = (y * (zh * jax.nn.sigmoid(zh))).astype(o_ref.dtype)


def _dn_out(o_f, o_b, p2, norm_g, *, z_col0):
    M, W = o_f.shape
    tm, tw = _tile(M, 512), _tile(W, 512)
    zj = z_col0 // tw
    return pl.pallas_call(
        _dn_out_body,
        grid=(M // tm, W // tw),
        in_specs=[pl.BlockSpec((tm, tw), lambda i, j: (i, j)),
                  pl.BlockSpec((tm, tw), lambda i, j: (i, j)),
                  pl.BlockSpec((tm, tw), lambda i, j: (i, zj + j)),
                  pl.BlockSpec((1, DN_V), lambda i, j: (0, 0))],
        out_specs=pl.BlockSpec((tm, tw), lambda i, j: (i, j)),
        out_shape=jax.ShapeDtypeStruct((M, W), BF16),
        compiler_params=_params(("parallel", "parallel")),
        name="dn_out",
    )(o_f, o_b, p2, norm_g.reshape(1, DN_V))


def _rope_table(positions):
    half = ROPE // 2
    inv_freq = ROPE_THETA ** (-jnp.arange(half, dtype=F32) / half)
    ang = positions.astype(F32)[..., None] * inv_freq
    cos, sin = jnp.cos(ang), jnp.sin(ang)
    return jnp.concatenate([cos, cos, sin, sin], axis=-1).reshape(-1, 4 * half)


def _mixer(u, tab, w_in, q_norm_g, kv_norm_g, w_uq, w_ukv, w_branch_a, conv_w, a_log, dt_bias,
           o_norm_g, w_branch_b, w_out, *, layer, batch, seq, stream, gate, alpha):
    d_model = u.shape[1]
    q_rank, kv_rank = q_norm_g.shape[0], kv_norm_g.shape[0]
    mla_heads = w_uq.shape[1] // MLA_QK
    dn_heads = a_log.shape[1]
    dn_qk_w, dn_v_w = dn_heads * DN_K, dn_heads * DN_V
    c_qkv = q_rank + kv_rank + ROPE
    c_z = c_qkv + 2 * dn_qk_w + dn_v_w
    c_a = c_z + dn_v_w
    c_gate = c_a + 4 * dn_heads
    assert ROPE + 4 * dn_heads == LANES and c_qkv - ROPE == (q_rank + kv_rank)
    w1 = jnp.concatenate([w_in[:, :c_qkv], w_in[:, c_a:c_gate]], axis=1).astype(BF16)
    w2 = jnp.concatenate([w_in[:, c_qkv:c_a], w_in[:, c_gate:]], axis=1).astype(BF16)
    p1 = _matmul(u, w1, F32, tm=512, tn=w1.shape[1])
    p2 = _matmul(u, w2, F32)

    wq = w_uq.reshape(q_rank, mla_heads, MLA_QK)
    r1, r2 = wq[..., NOPE:NOPE + ROPE // 2], wq[..., NOPE + ROPE // 2:]
    wq = jnp.concatenate([wq, -r2, r1], axis=-1).transpose(1, 0, 2).astype(BF16)
    wkv = w_ukv.reshape(kv_rank, mla_heads, NOPE + MLA_V).transpose(1, 0, 2).astype(BF16)
    q, k, v = _mla_prep(p1, tab, q_norm_g, kv_norm_g, wq, wkv, batch=batch, seq=seq)
    o_a = _flash(q, k, v).reshape(batch * seq, mla_heads * MLA_V)

    qkv = _dn_conv(p2, conv_w, batch=batch, seq=seq, heads=dn_heads)
    gates = _dn_gates(p1, a_log, dt_bias, batch=batch, seq=seq, heads=dn_heads,
                      col_block=(q_rank + kv_rank) // LANES)
    nc = seq // CHUNK
    gates = gates.reshape(batch, 2, 2, dn_heads, nc, CHUNK)
    grow = gates.transpose(0, 2, 3, 4, 1, 5).reshape(batch, 2, dn_heads, nc, 2 * CHUNK)
    g_cum = gates[:, 0]
    glast = jnp.stack([g_cum[:, 0, :, :, CHUNK - 1], g_cum[:, 1, :, :, 0]], axis=1)
    glast = jnp.broadcast_to(glast[..., None], glast.shape + (LANES,))
    u_, w_, qd, kd, qk = _dn_local(qkv, grow, glast, batch=batch, seq=seq, heads=dn_heads)
    o_f, o_b = _dn_scan(u_, w_, qd, kd, qk, glast, batch=batch, seq=seq, heads=dn_heads)
    o_bn = _dn_out(o_f, o_b, p2, o_norm_g, z_col0=2 * dn_qk_w + dn_v_w)

    merged = _merge(o_a, o_bn, _cast_layer(w_branch_a, layer), _cast_layer(w_branch_b, layer), p2, d_model,
                    gate_col0=2 * dn_qk_w + 2 * dn_v_w)
    return _matmul(merged, _cast_layer(w_out, layer), F32, stream=stream, gate=gate, seq=seq,
                   residual=(alpha, 1.0), columns_outer=True)


def _ffn(h, w_gate, w_up, w_down, layer, *, stream, gate, seq, alpha):
    a = _gateup(h, w_gate, w_up, layer)
    return _matmul(a, _cast_layer(w_down, layer), F32, tm=512, tn=1024, tk=w_down.shape[1],
                   stream=stream, gate=gate, seq=seq, residual=(alpha, 0.5), columns_outer=True)


def kernel(x, c, positions, ln_in_g, ln_in_b, w_ada, b_ada, ada_table, ffn1_w_gate, ffn1_w_up, ffn1_w_down, w_in, mla_q_norm_g, mla_kv_norm_g, mla_w_uq, mla_w_ukv, w_branch_a, dn_conv_w, dn_a_log, dn_dt_bias, dn_norm_g, w_branch_b, w_out, ffn2_w_gate, ffn2_w_up, ffn2_w_down, post_ln_g, post_ln_b):
    B, S, D = x.shape
    depth = ada_table.shape[0]
    alpha = (2.0 * depth) ** 0.25
    cond = _adaln(c, w_ada, b_ada).reshape(B, N_SUB, 3, D)
    mod = cond[None] + ada_table[:, None]
    tab = _rope_table(positions)
    post = functools.partial(_post, seq=S)

    xs, g, b = x.reshape(B * S, D), ln_in_g, ln_in_b
    h, stats = post(xs, g, b, mod[0, :, 0, 0], mod[0, :, 0, 1])
    for l in range(depth):
        xs = _ffn(h, ffn1_w_gate, ffn1_w_up, ffn1_w_down, l, stream=(xs, stats, g, b), gate=mod[l, :, 0, 2],
                  seq=S, alpha=alpha)
        g, b = post_ln_g[l, 0], post_ln_b[l, 0]
        h, stats = post(xs, g, b, mod[l, :, 1, 0], mod[l, :, 1, 1])
        xs = _mixer(h, tab, w_in[l], mla_q_norm_g[l], mla_kv_norm_g[l], mla_w_uq[l], mla_w_ukv[l],
                    w_branch_a, dn_conv_w[l], dn_a_log[l], dn_dt_bias[l], dn_norm_g[l],
                    w_branch_b, w_out, layer=l, batch=B, seq=S, stream=(xs, stats, g, b), gate=mod[l, :, 1, 2],
                    alpha=alpha)
        g, b = post_ln_g[l, 1], post_ln_b[l, 1]
        h, stats = post(xs, g, b, mod[l, :, 2, 0], mod[l, :, 2, 1])
        xs = _ffn(h, ffn2_w_gate, ffn2_w_up, ffn2_w_down, l, stream=(xs, stats, g, b), gate=mod[l, :, 2, 2],
                  seq=S, alpha=alpha)
        g, b = post_ln_g[l, 2], post_ln_b[l, 2]
        if l + 1 < depth:
            h, stats = post(xs, g, b, mod[l + 1, :, 0, 0], mod[l + 1, :, 0, 1])
    return post(xs, g, b, None, None).reshape(B, S, D)
```
